```python
import jax, jax.numpy as jnp
from jax import lax
import numpy as np

D_MODEL = 1024
BATCH = 32
SEQ = 256
DEPTH = 2
DEC_BATCH = 8
DEC_SEQ = 2048
PAST_LEN = 512

GRID_W = 64
N_HEADS = 8
N_KV_HEADS = 2
HEAD_DIM = 128
KV_REP = N_HEADS // N_KV_HEADS
ATTN_W = N_HEADS * HEAD_DIM
KV_W = N_KV_HEADS * HEAD_DIM
ROPE_THETA = 10000.0
Q_BLOCK = 128
CONV_W = D_MODEL
CONV_K = 3
D_INNER = 2 * D_MODEL
SSD_HEADDIM = 64
N_SSD_HEADS = D_INNER // SSD_HEADDIM
N_SSD_GROUPS = 8
SSD_HPG = N_SSD_HEADS // N_SSD_GROUPS
D_STATE = 128
SSD_CONV_DIM = D_INNER + 2 * N_SSD_GROUPS * D_STATE
CHUNK = 128
D_FF = -(-8 * D_MODEL // 768) * 256
N_BRANCH = 3
IN_SIZES = (ATTN_W, KV_W, KV_W, CONV_W, CONV_W, CONV_W, D_INNER, SSD_CONV_DIM, 2 * N_SSD_HEADS, N_BRANCH * D_MODEL)
IN_DIM = sum(IN_SIZES)
EPS = 1e-6

kernel_name = 'hybrid_gated_attn_conv_ssd_diffusion_step'


def rms_norm(x, g):
    xf = x.astype(jnp.float32)
    y = xf * lax.rsqrt(jnp.mean(jnp.square(xf), axis=-1, keepdims=True) + EPS)
    return (y * g.astype(jnp.float32)).astype(x.dtype)


def dwconv_centred(x, w):
    return lax.conv_general_dilated(x, w[:, None, :].astype(x.dtype), window_strides=(1,),
                                    padding=[(CONV_K // 2, CONV_K // 2)],
                                    dimension_numbers=('NWC', 'WIO', 'NWC'),
                                    feature_group_count=x.shape[-1])


def axial_rope(n_tok):
    rows = n_tok // GRID_W
    row = jnp.repeat(jnp.arange(rows, dtype=jnp.float32), GRID_W)
    col = jnp.tile(jnp.arange(GRID_W, dtype=jnp.float32), rows)
    axis_dim = HEAD_DIM // 2
    inv_freq = 1.0 / (ROPE_THETA ** (jnp.arange(0, axis_dim, 2, dtype=jnp.float32) / axis_dim))
    ang = jnp.concatenate([row[:, None] * inv_freq, col[:, None] * inv_freq], axis=-1)
    return jnp.cos(ang), jnp.sin(ang)


def apply_rope(x, cos, sin):
    xp = x.reshape(x.shape[:-1] + (HEAD_DIM // 2, 2))
    xe, xo = xp[..., 0], xp[..., 1]
    cs, sn = cos[None, :, None, :], sin[None, :, None, :]
    return jnp.stack([xe * cs - xo * sn, xe * sn + xo * cs], axis=-1).reshape(x.shape)


def blocked_attention(q, keys, vals):
    b, t = q.shape[:2]
    nblk = t // Q_BLOCK
    qb = q.reshape(b, nblk, Q_BLOCK, N_KV_HEADS, KV_REP, HEAD_DIM).transpose(1, 0, 2, 3, 4, 5)
    scale = HEAD_DIM ** -0.5

    def one_block(qblk):
        s = jnp.einsum('bqgrd,bsgd->bgrqs', qblk, keys) * scale
        p = jax.nn.softmax(s, axis=-1)
        return jnp.einsum('bgrqs,bsgd->bqgrd', p, vals)

    out = lax.map(one_block, qb)
    return out.transpose(1, 0, 2, 3, 4, 5).reshape(b, t, ATTN_W)


def ssd_chunked(x, dt, a, bm, cm, s0):
    b, l = x.shape[:2]
    nc = l // CHUNK
    g, r, p, n = N_SSD_GROUPS, SSD_HPG, SSD_HEADDIM, D_STATE
    acs = jnp.cumsum((dt * a).reshape(b, nc, CHUNK, g, r), axis=2)
    xd = (x * dt[..., None]).reshape(b, nc, CHUNK, g, r, p)
    bc = bm.reshape(b, nc, CHUNK, g, n)
    cc = cm.reshape(b, nc, CHUNK, g, n)
    seg = acs[:, :, :, None] - acs[:, :, None]
    causal = jnp.tril(jnp.ones((CHUNK, CHUNK), dtype=bool))[:, :, None, None]
    decay = jnp.exp(jnp.where(causal, seg, -jnp.inf))
    cb = jnp.einsum('bcign,bcjgn->bcijg', cc, bc)
    y_diag = jnp.einsum('bcijgr,bcjgrp->bcigrp', cb[..., None] * decay, xd)
    to_end = jnp.exp(acs[:, :, -1:] - acs)
    states = jnp.einsum('bcjgn,bcjgrp->bcgrpn', bc, xd * to_end[..., None])
    chunk_decay = jnp.exp(acs[:, :, -1])

    def step(s, inp):
        st, dec = inp
        return s * dec[..., None, None] + st, s

    final, prev = lax.scan(step, s0, (states.transpose(1, 0, 2, 3, 4, 5), chunk_decay.transpose(1, 0, 2, 3)))
    prev = prev.transpose(1, 0, 2, 3, 4, 5)
    y_off = jnp.einsum('bcign,bcgrpn->bcigrp', cc, prev) * jnp.exp(acs)[..., None]
    return (y_diag + y_off).reshape(b, l, g, r, p), final


def token_mixers(h, latent, ctx_k, ctx_v, ssd_s0, w_in, q_g, k_g, w_attn_o, conv_w, w_conv_o,
                 ssd_conv_w, ssd_conv_b, ssd_dt_bias, ssd_a_log, ssd_d, ssd_norm_g, w_ssd_o, w_merge):
    f32 = jnp.float32
    b, t, _ = h.shape
    proj = h @ w_in
    (q, k, v, conv_bg, conv_cg, conv_x, z, xbc, dt_raw, gate_logits) = jnp.split(
        proj, np.cumsum(IN_SIZES)[:-1].tolist(), axis=-1)
    q = rms_norm(q.reshape(b, t, N_HEADS, HEAD_DIM), q_g).astype(f32)
    k = rms_norm(k.reshape(b, t, N_KV_HEADS, HEAD_DIM), k_g)
    v = v.reshape(b, t, N_KV_HEADS, HEAD_DIM)
    if latent:
        cos, sin = axial_rope(t)
        q = apply_rope(q, cos, sin)
        keys = jnp.concatenate([apply_rope(k.astype(f32), cos, sin), ctx_k.astype(f32)], axis=1)
        vals = jnp.concatenate([v.astype(f32), ctx_v.astype(f32)], axis=1)
    else:
        keys, vals = k.astype(f32), v.astype(f32)
    y_attn = blocked_attention(q, keys, vals).astype(h.dtype) @ w_attn_o
    y_conv = (conv_bg * dwconv_centred(conv_cg * conv_x, conv_w)) @ w_conv_o
    xbc = jax.nn.silu(dwconv_centred(xbc, ssd_conv_w) + ssd_conv_b)
    xs, bm, cm = jnp.split(xbc, [D_INNER, D_INNER + N_SSD_GROUPS * D_STATE], axis=-1)
    xs = xs.reshape(b, t, N_SSD_GROUPS, SSD_HPG, SSD_HEADDIM).astype(f32)
    bm = bm.reshape(b, t, N_SSD_GROUPS, D_STATE).astype(f32)
    cm = cm.reshape(b, t, N_SSD_GROUPS, D_STATE).astype(f32)
    dt = jax.nn.softplus(dt_raw.astype(f32).reshape(b, t, 2, N_SSD_GROUPS, SSD_HPG)
                         + ssd_dt_bias.astype(f32).reshape(2, N_SSD_GROUPS, SSD_HPG))
    a = -jnp.exp(ssd_a_log.astype(f32)).reshape(2, N_SSD_GROUPS, SSD_HPG)
    s0 = ssd_s0.astype(f32).reshape(b, 2, N_SSD_GROUPS, SSD_HPG, SSD_HEADDIM, D_STATE)
    y_f, s_f = ssd_chunked(xs, dt[:, :, 0], a[0], bm, cm, s0[:, 0])
    y_b, s_b = ssd_chunked(xs[:, ::-1], dt[:, ::-1, 1], a[1], bm[:, ::-1], cm[:, ::-1], s0[:, 1])
    y = y_f + y_b[:, ::-1] + ssd_d.astype(f32).reshape(N_SSD_GROUPS, SSD_HPG, 1) * xs
    y = y.reshape(b, t, D_INNER).astype(h.dtype)
    y_ssd = rms_norm(y * jax.nn.silu(z), ssd_norm_g) @ w_ssd_o
    g_attn, g_conv, g_ssd = jnp.split(jax.nn.sigmoid(gate_logits), N_BRANCH, axis=-1)
    out = (g_attn * y_attn + g_conv * y_conv + g_ssd * y_ssd) @ w_merge
    if latent:
        return out, None, None, None
    s_final = jnp.stack([s_f, s_b], axis=1).reshape(b, 2, N_SSD_HEADS, SSD_HEADDIM, D_STATE)
    return out, k, v, s_final


def trunk_layer(x, mod, latent, ctx_k, ctx_v, ssd_s0, norm1_g, norm2_g, w_in, q_g, k_g, w_attn_o, conv_w,
                w_conv_o, ssd_conv_w, ssd_conv_b, ssd_dt_bias, ssd_a_log, ssd_d, ssd_norm_g, w_ssd_o,
                w_merge, ffn_w1, ffn_w2):
    shift1, scale1, gate1, shift2, scale2, gate2 = jnp.split(mod, 6, axis=-1)
    h = rms_norm(x, norm1_g) * (1.0 + scale1) + shift1
    mixed, k, v, s = token_mixers(h, latent, ctx_k, ctx_v, ssd_s0, w_in, q_g, k_g, w_attn_o, conv_w,
                                  w_conv_o, ssd_conv_w, ssd_conv_b, ssd_dt_bias, ssd_a_log, ssd_d,
                                  ssd_norm_g, w_ssd_o, w_merge)
    x = x + gate1 * mixed
    h = rms_norm(x, norm2_g) * (1.0 + scale2) + shift2
    hg, hu = jnp.split(h @ ffn_w1, 2, axis=-1)
    x = x + gate2 * ((jax.nn.silu(hg) * hu) @ ffn_w2)
    return x, k, v, s


def setup_inputs(seed: int = 0) -> dict:
    key = jax.random.key(seed)
    ks = iter(jax.random.split(key, 40))
    f32 = jnp.float32

    def nrm(shape, scale):
        return jax.random.normal(next(ks), shape, f32) * scale

    dt0 = jnp.exp(jax.random.uniform(next(ks), (DEPTH, 2, N_SSD_HEADS), f32, np.log(1e-3), np.log(1e-1)))
    return {
        'x_prompt': nrm((BATCH, SEQ, D_MODEL), 1.0),
        'x_sample': nrm((DEC_BATCH, DEC_SEQ, D_MODEL), 1.0),
        'c': nrm((DEC_BATCH, D_MODEL), 1.0),
        'cache_k': nrm((DEC_BATCH, DEPTH, PAST_LEN, N_KV_HEADS, HEAD_DIM), 1.0),
        'cache_v': nrm((DEC_BATCH, DEPTH, PAST_LEN, N_KV_HEADS, HEAD_DIM), 1.0),
        'state_ssd': nrm((DEC_BATCH, DEPTH, 2, N_SSD_HEADS, SSD_HEADDIM, D_STATE), 0.1),
        'c_ctx': nrm((D_MODEL,), 1.0),
        'ada_w': nrm((DEPTH, D_MODEL, 6 * D_MODEL), 0.5 * D_MODEL ** -0.5),
        'ada_b': nrm((DEPTH, 6 * D_MODEL), 0.02),
        'norm1_g': 1.0 + nrm((DEPTH, D_MODEL), 0.02),
        'norm2_g': 1.0 + nrm((DEPTH, D_MODEL), 0.02),
        'w_in': nrm((DEPTH, D_MODEL, IN_DIM), D_MODEL ** -0.5),
        'q_norm_g': 1.0 + nrm((DEPTH, HEAD_DIM), 0.02),
        'k_norm_g': 1.0 + nrm((DEPTH, HEAD_DIM), 0.02),
        'w_attn_o': nrm((DEPTH, ATTN_W, D_MODEL), ATTN_W ** -0.5),
        'conv_w': nrm((DEPTH, CONV_K, CONV_W), CONV_K ** -0.5),
        'w_conv_o': nrm((DEPTH, CONV_W, D_MODEL), CONV_W ** -0.5),
        'ssd_conv_w': nrm((DEPTH, CONV_K, SSD_CONV_DIM), CONV_K ** -0.5),
        'ssd_conv_b': nrm((DEPTH, SSD_CONV_DIM), 0.02),
        'ssd_dt_bias': dt0 + jnp.log(-jnp.expm1(-dt0)),
        'ssd_a_log': jnp.log(jax.random.uniform(next(ks), (DEPTH, 2, N_SSD_HEADS), f32, 1.0, 16.0)),
        'ssd_d': 1.0 + nrm((DEPTH, N_SSD_HEADS), 0.1),
        'ssd_norm_g': 1.0 + nrm((DEPTH, D_INNER), 0.02),
        'w_ssd_o': nrm((DEPTH, D_INNER, D_MODEL), D_INNER ** -0.5),
        'w_merge': nrm((DEPTH, D_MODEL, D_MODEL), D_MODEL ** -0.5),
        'ffn_w1': nrm((DEPTH, D_MODEL, 2 * D_FF), D_MODEL ** -0.5),
        'ffn_w2': nrm((DEPTH, D_FF, D_MODEL), D_FF ** -0.5),
    }


def reference(x_prompt, x_sample, c, cache_k, cache_v, state_ssd, c_ctx, ada_w, ada_b, norm1_g, norm2_g,
              w_in, q_norm_g, k_norm_g, w_attn_o, conv_w, w_conv_o, ssd_conv_w, ssd_conv_b, ssd_dt_bias,
              ssd_a_log, ssd_d, ssd_norm_g, w_ssd_o, w_merge, ffn_w1, ffn_w2):
    y_prompt, y_sample = x_prompt, x_sample
    silu_ctx = jax.nn.silu(c_ctx)[None, :]
    silu_c = jax.nn.silu(c)
    zero_state = jnp.zeros((x_prompt.shape[0], 2, N_SSD_HEADS, SSD_HEADDIM, D_STATE), jnp.float32)
    ks_out, vs_out, ss_out = [], [], []
    for l in range(DEPTH):
        mod_ctx = (silu_ctx @ ada_w[l] + ada_b[l])[:, None, :]
        mod_lat = (silu_c @ ada_w[l] + ada_b[l])[:, None, :]
        weights = (norm1_g[l], norm2_g[l], w_in[l], q_norm_g[l], k_norm_g[l], w_attn_o[l], conv_w[l],
                   w_conv_o[l], ssd_conv_w[l], ssd_conv_b[l], ssd_dt_bias[l], ssd_a_log[l], ssd_d[l],
                   ssd_norm_g[l], w_ssd_o[l], w_merge[l], ffn_w1[l], ffn_w2[l])
        y_prompt, k_l, v_l, s_l = trunk_layer(y_prompt, mod_ctx, False, None, None, zero_state, *weights)
        y_sample, _, _, _ = trunk_layer(y_sample, mod_lat, True, cache_k[:, l], cache_v[:, l],
                                        state_ssd[:, l], *weights)
        ks_out.append(k_l)
        vs_out.append(v_l)
        ss_out.append(s_l)
    new_k = jnp.stack(ks_out, axis=1)
    new_v = jnp.stack(vs_out, axis=1)
    new_ssd = jnp.stack(ss_out, axis=1)
    return (y_prompt, y_sample, new_k, new_v, new_ssd)
```

```python
import functools

import jax
import jax.numpy as jnp
import numpy as np
from jax import lax
from jax.experimental import pallas as pl
from jax.experimental.pallas import tpu as pltpu

F32 = jnp.float32
BF16 = jnp.bfloat16

D_MODEL = 1024
DEPTH = 2
GRID_W = 64
N_HEADS = 8
N_KV_HEADS = 2
HEAD_DIM = 128
KV_REP = N_HEADS // N_KV_HEADS
ROPE_THETA = 10000.0
CONV_K = 3
D_INNER = 2 * D_MODEL
SSD_HEADDIM = 64
N_SSD_HEADS = D_INNER // SSD_HEADDIM
N_SSD_GROUPS = 8
SSD_HPG = N_SSD_HEADS // N_SSD_GROUPS
D_STATE = 128
CHUNK = 128
D_FF = -(-8 * D_MODEL // 768) * 256
EPS = 1e-6

LANES = 128
BF16_ROWS = 16
VMEM_LIMIT = 56 * 1024 * 1024

C_Z = 0
C_XBC = C_Z + D_INNER
C_GATE = C_XBC + 2 * D_INNER
C_BG = C_GATE + 3 * D_MODEL
C_CG = C_BG + D_MODEL
C_CX = C_CG + D_MODEL
C_Q = C_CX + D_MODEL
C_K = C_Q + D_MODEL
C_V = C_K + N_KV_HEADS * HEAD_DIM
N_MAIN = C_V + N_KV_HEADS * HEAD_DIM
PROJ_TN = 512
J_Q0 = C_Q // PROJ_TN
J_KV = C_K // PROJ_TN
MOD_ROWS = 16
CTX_ROW = 0
LAT_ROW0 = 1


def _cparams(sem):
    return pltpu.CompilerParams(dimension_semantics=sem, vmem_limit_bytes=VMEM_LIMIT)


def _dot(a, b):
    return jnp.dot(a, b, preferred_element_type=F32)


def _sigmoid(x):
    return 1.0 / (1.0 + jnp.exp(-x))


def _silu(x):
    return x * _sigmoid(x)


def _softplus(x):
    return jnp.maximum(x, 0.0) + jnp.log1p(jnp.exp(-jnp.abs(x)))


def _split3(x):
    hi = x.astype(BF16)
    r1 = x - hi.astype(F32)
    mid = r1.astype(BF16)
    lo = (r1 - mid.astype(F32)).astype(BF16)
    return hi, mid, lo


def _mod_kernel(c_ref, w_ref, b_ref, o_ref):
    s = _silu(c_ref[...]).astype(BF16)
    o_ref[...] = _dot(s, w_ref[...].astype(BF16)) + b_ref[...]


def _modulation(cond, ada_w, ada_b):
    tn = 1536
    return pl.pallas_call(
        _mod_kernel,
        out_shape=jax.ShapeDtypeStruct((DEPTH, MOD_ROWS, 6 * D_MODEL), F32),
        grid=(DEPTH, 6 * D_MODEL // tn),
        in_specs=[
            pl.BlockSpec((MOD_ROWS, D_MODEL), lambda l, j: (0, 0)),
            pl.BlockSpec((None, D_MODEL, tn), lambda l, j: (l, 0, j)),
            pl.BlockSpec((None, 1, tn), lambda l, j: (l, 0, j)),
        ],
        out_specs=pl.BlockSpec((None, MOD_ROWS, tn), lambda l, j: (l, 0, j)),
        compiler_params=_cparams(("parallel", "parallel")),
        name="modulation",
    )(cond, ada_w, ada_b.reshape(DEPTH, 1, 6 * D_MODEL))


def _mod_spec(layer, which, row_fn):
    return pl.BlockSpec((None, None, 1, D_MODEL), lambda *ids: (layer, row_fn(*ids), 0, which))


def _norm_mod(x, g, scale, shift):
    ms = jnp.mean(x * x, axis=-1, keepdims=True)
    return (x * lax.rsqrt(ms + EPS) * g) * (1.0 + scale) + shift


def _head_norm(a, g):
    ms = jnp.mean(a * a, axis=-1, keepdims=True)
    return a * lax.rsqrt(ms + EPS) * g


def _rope(a, cos, sin_signed):
    lane = lax.broadcasted_iota(jnp.int32, a.shape, 1)
    nxt = pltpu.roll(a, HEAD_DIM - 1, 1)
    prv = pltpu.roll(a, 1, 1)
    swapped = jnp.where(lane % 2 == 0, nxt, prv)
    return a * cos + swapped * sin_signed


def _inproj_kernel(latent, x_ref, shift_ref, scale_ref, g_ref, w_ref, wdt_ref, qg_ref, kg_ref, *rest):
    if latent:
        cos_ref, sin_ref, proj_ref, dt_ref, h_scr = rest
    else:
        proj_ref, dt_ref, k32_ref, v32_ref, h_scr = rest
    j = pl.program_id(1)

    @pl.when(j == 0)
    def _():
        h = _norm_mod(x_ref[...], g_ref[...], scale_ref[...], shift_ref[...]).astype(BF16)
        h_scr[...] = h
        dt_ref[...] = _dot(h, wdt_ref[...])

    acc = _dot(h_scr[...], w_ref[...])

    @pl.when(j < J_Q0)
    def _():
        proj_ref[...] = acc.astype(BF16)

    @pl.when(jnp.logical_and(j >= J_Q0, j < J_KV))
    def _():
        for hh in range(PROJ_TN // HEAD_DIM):
            sl = slice(hh * HEAD_DIM, (hh + 1) * HEAD_DIM)
            q = _head_norm(acc[:, sl], qg_ref[...])
            if latent:
                q = _rope(q, cos_ref[...], sin_ref[...])
            proj_ref[:, sl] = (q * (HEAD_DIM ** -0.5)).astype(BF16)

    @pl.when(j == J_KV)
    def _():
        for hh in range(N_KV_HEADS):
            sl = slice(hh * HEAD_DIM, (hh + 1) * HEAD_DIM)
            k = _head_norm(acc[:, sl], kg_ref[...])
            if latent:
                k = _rope(k, cos_ref[...], sin_ref[...])
            else:
                k32_ref[:, sl] = k
            proj_ref[:, sl] = k.astype(BF16)
        v = acc[:, N_KV_HEADS * HEAD_DIM:]
        if not latent:
            v32_ref[...] = v
        proj_ref[:, N_KV_HEADS * HEAD_DIM:] = v.astype(BF16)


def _in_proj(x, mod4, layer, seq, latent, norm_g, w_main, w_dt, q_g, k_g, rope_tabs):
    t = x.shape[0]
    tm = 1024
    tiles_per_seq = max(seq // tm, 1)
    row_fn = (lambda i, j: LAT_ROW0 + i // tiles_per_seq) if latent else (lambda i, j: CTX_ROW)
    in_specs = [
        pl.BlockSpec((tm, D_MODEL), lambda i, j: (i, 0)),
        _mod_spec(layer, 0, row_fn),
        _mod_spec(layer, 1, row_fn),
        pl.BlockSpec((1, D_MODEL), lambda i, j: (0, 0)),
        pl.BlockSpec((D_MODEL, PROJ_TN), lambda i, j: (0, j)),
        pl.BlockSpec((D_MODEL, LANES), lambda i, j: (0, 0)),
        pl.BlockSpec((1, HEAD_DIM), lambda i, j: (0, 0)),
        pl.BlockSpec((1, HEAD_DIM), lambda i, j: (0, 0)),
    ]
    args = [x, mod4, mod4, norm_g, w_main, w_dt, q_g, k_g]
    out_shape = [jax.ShapeDtypeStruct((t, N_MAIN), BF16), jax.ShapeDtypeStruct((t, LANES), F32)]
    out_specs = [pl.BlockSpec((tm, PROJ_TN), lambda i, j: (i, j)), pl.BlockSpec((tm, LANES), lambda i, j: (i, 0))]
    if latent:
        tab_spec = pl.BlockSpec((tm, HEAD_DIM), lambda i, j: (i % tiles_per_seq, 0))
        in_specs += [tab_spec, tab_spec]
        args += list(rope_tabs)
    else:
        kv_w = N_KV_HEADS * HEAD_DIM
        out_shape += [jax.ShapeDtypeStruct((t, kv_w), F32)] * 2
        out_specs += [pl.BlockSpec((tm, kv_w), lambda i, j: (i, 0))] * 2
    return pl.pallas_call(
        functools.partial(_inproj_kernel, latent),
        out_shape=out_shape,
        grid=(t // tm, N_MAIN // PROJ_TN),
        in_specs=in_specs,
        out_specs=out_specs,
        scratch_shapes=[pltpu.VMEM((tm, D_MODEL), BF16)],
        compiler_params=_cparams(("parallel", "arbitrary")),
        name="in_proj_lat" if latent else "in_proj_ctx",
    )(*args)


def _attn_kernel(latent, q_ref, k_ref, v_ref, *rest):
    if latent:
        ck_ref, cv_ref, o_ref = rest
        ck = ck_ref[...].astype(BF16)
        cv = cv_ref[...].astype(BF16)
    else:
        (o_ref,) = rest
    k = k_ref[...]
    v = v_ref[...]
    nt = (((1,), (1,)), ((), ()))
    for hh in range(KV_REP):
        sl = slice(hh * HEAD_DIM, (hh + 1) * HEAD_DIM)
        q = q_ref[:, sl]
        s = lax.dot_general(q, k, nt, preferred_element_type=F32)
        m = jnp.max(s, axis=-1, keepdims=True)
        if latent:
            s2 = lax.dot_general(q, ck, nt, preferred_element_type=F32)
            m = jnp.maximum(m, jnp.max(s2, axis=-1, keepdims=True))
        p = jnp.exp(s - m)
        den = jnp.sum(p, axis=-1, keepdims=True)
        o = _dot(p.astype(BF16), v)
        if latent:
            p2 = jnp.exp(s2 - m)
            den = den + jnp.sum(p2, axis=-1, keepdims=True)
            o = o + _dot(p2.astype(BF16), cv)
        o_ref[:, sl] = (o / den).astype(BF16)


def _attention(proj, seq, latent, layer, cache_k4, cache_v4):
    t = proj.shape[0]
    nb = t // seq
    tq = 256
    qt = seq // tq
    gw = KV_REP * HEAD_DIM
    in_specs = [
        pl.BlockSpec((tq, gw), lambda b, g, i: (b * qt + i, C_Q // gw + g)),
        pl.BlockSpec((seq, HEAD_DIM), lambda b, g, i: (b, C_K // HEAD_DIM + g)),
        pl.BlockSpec((seq, HEAD_DIM), lambda b, g, i: (b, C_V // HEAD_DIM + g)),
    ]
    args = [proj, proj, proj]
    if latent:
        past = cache_k4.shape[2]
        cspec = pl.BlockSpec((None, None, past, HEAD_DIM), lambda b, g, i: (b, layer, 0, g))
        in_specs += [cspec, cspec]
        args += [cache_k4, cache_v4]
    return pl.pallas_call(
        functools.partial(_attn_kernel, latent),
        out_shape=jax.ShapeDtypeStruct((t, N_HEADS * HEAD_DIM), BF16),
        grid=(nb, N_KV_HEADS, qt),
        in_specs=in_specs,
        out_specs=pl.BlockSpec((tq, gw), lambda b, g, i: (b * qt + i, g)),
        compiler_params=_cparams(("parallel", "parallel", "arbitrary")),
        name="attention_lat" if latent else "attention_ctx",
    )(*args)


def _conv3(p, prev_row, next_row, w_ref, first, last):
    tm = p.shape[0]
    rid = lax.broadcasted_iota(jnp.int32, p.shape, 0)
    prev_row = jnp.where(first, 0.0, prev_row)
    next_row = jnp.where(last, 0.0, next_row)
    dn = jnp.where(rid == 0, prev_row, pltpu.roll(p, 1, 0))
    up = jnp.where(rid == tm - 1, next_row, pltpu.roll(p, tm - 1, 0))
    return w_ref[0:1, :] * dn + w_ref[1:2, :] * p + w_ref[2:3, :] * up


def _tile_edges(tm, seq):
    i = pl.program_id(0)
    tps = seq // tm
    return (i % tps) == 0, (i % tps) == tps - 1


def _gconv_kernel(tm, seq, bg_ref, cg_ref, cx_ref, cgp_ref, cxp_ref, cgn_ref, cxn_ref, w_ref, o_ref):
    first, last = _tile_edges(tm, seq)
    p = cg_ref[...].astype(F32) * cx_ref[...].astype(F32)
    r = BF16_ROWS - 1
    prev_row = cgp_ref[r:r + 1, :].astype(F32) * cxp_ref[r:r + 1, :].astype(F32)
    next_row = cgn_ref[0:1, :].astype(F32) * cxn_ref[0:1, :].astype(F32)
    conv = _conv3(p, prev_row, next_row, w_ref, first, last)
    o_ref[...] = (bg_ref[...].astype(F32) * conv).astype(BF16)


def _halo_maps(tm, nrows, width_blocks_fn):
    rpt = tm // BF16_ROWS
    nblk = nrows // BF16_ROWS

    def prev_map(i, *rest):
        return (jnp.maximum(i * rpt - 1, 0), width_blocks_fn(*rest))

    def next_map(i, *rest):
        return (jnp.minimum((i + 1) * rpt, nblk - 1), width_blocks_fn(*rest))

    return prev_map, next_map


def _gconv(proj, seq, conv_w):
    t = proj.shape[0]
    tm = min(512, seq)
    c = D_MODEL

    def spec_main(cb):
        return pl.BlockSpec((tm, c), lambda i: (i, cb))

    def halo(cb):
        pm, nm = _halo_maps(tm, t, lambda: cb)
        return pl.BlockSpec((BF16_ROWS, c), pm), pl.BlockSpec((BF16_ROWS, c), nm)

    cgp, cgn = halo(C_CG // c)
    cxp, cxn = halo(C_CX // c)
    return pl.pallas_call(
        functools.partial(_gconv_kernel, tm, seq),
        out_shape=jax.ShapeDtypeStruct((t, c), BF16),
        grid=(t // tm,),
        in_specs=[spec_main(C_BG // c), spec_main(C_CG // c), spec_main(C_CX // c), cgp, cxp, cgn, cxn,
                  pl.BlockSpec((CONV_K, c), lambda i: (0, 0))],
        out_specs=pl.BlockSpec((tm, c), lambda i: (i, 0)),
        compiler_params=_cparams(("parallel",)),
        name="gated_conv",
    )(proj, proj, proj, proj, proj, proj, proj, conv_w)


def _ssdconv_kernel(tm, seq, x_ref, xp_ref, xn_ref, w_ref, b_ref, o_ref):
    first, last = _tile_edges(tm, seq)
    r = BF16_ROWS - 1
    conv = _conv3(x_ref[...].astype(F32), xp_ref[r:r + 1, :].astype(F32), xn_ref[0:1, :].astype(F32),
                  w_ref, first, last)
    o_ref[...] = _silu(conv + b_ref[...]).astype(BF16)


def _ssd_conv(proj, seq, w, b):
    t = proj.shape[0]
    tm = min(512, seq)
    c = 1024
    nc = 2 * D_INNER // c
    off = C_XBC // c
    pm, nm = _halo_maps(tm, t, lambda j: off + j)
    return pl.pallas_call(
        functools.partial(_ssdconv_kernel, tm, seq),
        out_shape=jax.ShapeDtypeStruct((t, 2 * D_INNER), BF16),
        grid=(t // tm, nc),
        in_specs=[pl.BlockSpec((tm, c), lambda i, j: (i, off + j)),
                  pl.BlockSpec((BF16_ROWS, c), pm), pl.BlockSpec((BF16_ROWS, c), nm),
                  pl.BlockSpec((CONV_K, c), lambda i, j: (0, j)),
                  pl.BlockSpec((1, c), lambda i, j: (0, j))],
        out_specs=pl.BlockSpec((tm, c), lambda i, j: (i, j)),
        compiler_params=_cparams(("parallel", "parallel")),
        name="ssd_conv",
    )(proj, proj, proj, w, b)


GW = SSD_HPG * SSD_HEADDIM


def _ssd_kernel(rev, nchunks, xs_ref, bm_ref, cm_ref, dtr_ref, dtrt_ref, brow_ref, bcol_ref, arow_ref, acol_ref,
                s0_ref, e3_ref, d_ref, y_ref, sfin_ref, st_scr):
    c = pl.program_id(1)

    @pl.when(c == 0)
    def _():
        st_scr[...] = s0_ref[...]

    ii = lax.broadcasted_iota(jnp.int32, (CHUNK, CHUNK), 0)
    jj = lax.broadcasted_iota(jnp.int32, (CHUNK, CHUNK), 1)
    keep = (ii <= jj) if rev else (ii >= jj)
    tri = jnp.where(keep, 1.0, 0.0).astype(BF16)
    tri_t = jnp.where((jj <= ii) if rev else (jj >= ii), 1.0, 0.0).astype(BF16)

    dt = _softplus(dtr_ref[...] + brow_ref[...])
    dta = dt * (-jnp.exp(arow_ref[...]))
    acs = sum(_dot(tri, part) for part in _split3(dta))
    dt_t = _softplus(dtrt_ref[...] + bcol_ref[...])
    dta_t = dt_t * (-jnp.exp(acol_ref[...]))
    acs_t = sum(_dot(part, tri_t) for part in _split3(dta_t))

    edge = 0 if rev else CHUNK - 1
    tot = acs[edge:edge + 1, :]
    dt3 = jnp.concatenate(_split3(dt), axis=1)
    ea3 = jnp.concatenate(_split3(jnp.exp(acs)), axis=1)
    te3 = jnp.concatenate(_split3(jnp.exp(tot - acs)), axis=1)
    head_of_lane = lax.broadcasted_iota(jnp.int32, (CHUNK, GW), 1) // SSD_HEADDIM
    lane0 = N_SSD_HEADS if rev else 0

    for g in range(N_SSD_GROUPS):
        gs = slice(g * GW, (g + 1) * GW)
        ns = slice(g * D_STATE, (g + 1) * D_STATE)
        e3 = e3_ref[:, gs]
        dt_w = _dot(dt3, e3)
        ea_w = _dot(ea3, e3)
        te_w = _dot(te3, e3)
        xs = xs_ref[:, gs].astype(F32)
        bc = bm_ref[:, ns]
        cc = cm_ref[:, ns]
        bc_t = bc.astype(F32).T.astype(BF16)
        cb = _dot(cc, bc_t)
        xd = xs * dt_w
        xd_b = xd.astype(BF16)
        st = st_scr[:, gs]
        y = _dot(cc, st.astype(BF16)) * ea_w
        for r in range(SSD_HPG):
            ln = lane0 + g * SSD_HPG + r
            seg = acs[:, ln:ln + 1] - acs_t[ln:ln + 1, :]
            m = (cb * jnp.exp(jnp.where(keep, seg, -jnp.inf))).astype(BF16)
            y = y + _dot(m, jnp.where(head_of_lane == r, xd_b, jnp.zeros_like(xd_b)))
        if not rev:
            y = y + d_ref[:, gs] * xs
        y_ref[:, gs] = y.astype(BF16)
        st_scr[:, gs] = st * ea_w[edge:edge + 1, :] + _dot(bc_t, (xd * te_w).astype(BF16))

    @pl.when(c == nchunks - 1)
    def _():
        sfin_ref[...] = st_scr[...]


def _ssd(xact, dt_raw, dt_raw_t, seq, rev, bias, a_log, s0_t, e3, d_wide):
    t = xact.shape[0]
    nb = t // seq
    nchunks = seq // CHUNK

    def row(b, c):
        return b * nchunks + ((nchunks - 1 - c) if rev else c)

    d = 1 if rev else 0
    pad = jnp.zeros((LANES - 2 * N_SSD_HEADS,), F32)
    b_row = jnp.concatenate([bias.reshape(-1), pad]).reshape(1, LANES)
    a_row = jnp.concatenate([a_log.reshape(-1), pad]).reshape(1, LANES)
    const = lambda b, c: (0, 0)
    y, sfin = pl.pallas_call(
        functools.partial(_ssd_kernel, rev, nchunks),
        out_shape=[jax.ShapeDtypeStruct((t, D_INNER), BF16), jax.ShapeDtypeStruct((nb, D_STATE, D_INNER), F32)],
        grid=(nb, nchunks),
        in_specs=[
            pl.BlockSpec((CHUNK, D_INNER), lambda b, c: (row(b, c), 0)),
            pl.BlockSpec((CHUNK, N_SSD_GROUPS * D_STATE), lambda b, c: (row(b, c), 2)),
            pl.BlockSpec((CHUNK, N_SSD_GROUPS * D_STATE), lambda b, c: (row(b, c), 3)),
            pl.BlockSpec((CHUNK, LANES), lambda b, c: (row(b, c), 0)),
            pl.BlockSpec((LANES, CHUNK), lambda b, c: (0, row(b, c))),
            pl.BlockSpec((1, LANES), const), pl.BlockSpec((LANES, 1), const),
            pl.BlockSpec((1, LANES), const), pl.BlockSpec((LANES, 1), const),
            pl.BlockSpec((None, None, D_STATE, D_INNER), lambda b, c: (b, d, 0, 0)),
            pl.BlockSpec((3 * LANES, D_INNER), const),
            pl.BlockSpec((1, D_INNER), const),
        ],
        out_specs=[pl.BlockSpec((CHUNK, D_INNER), lambda b, c: (row(b, c), 0)),
                   pl.BlockSpec((None, D_STATE, D_INNER), lambda b, c: (b, 0, 0))],
        scratch_shapes=[pltpu.VMEM((D_STATE, D_INNER), F32)],
        compiler_params=_cparams(("parallel", "arbitrary")),
        name="ssd_bwd" if rev else "ssd_fwd",
    )(xact, xact, xact, dt_raw, dt_raw_t, b_row, b_row.reshape(LANES, 1), a_row, a_row.reshape(LANES, 1),
      s0_t, e3, d_wide)
    return y, sfin


def _expansion_matrix(rev):
    lane = np.arange(LANES)[:, None]
    head = np.arange(D_INNER)[None, :] // SSD_HEADDIM + (N_SSD_HEADS if rev else 0)
    e = (lane == head).astype(np.float32)
    return jnp.asarray(np.concatenate([e, e, e], axis=0), dtype=BF16)


def _merge_kernel(x_ref, gate_ref, attn_ref, u_ref, yf_ref, yb_ref, z_ref, gl_ref, ng_ref,
                  wa_ref, wc_ref, ws_ref, wm_ref, o_ref):
    y = yf_ref[...].astype(F32) + yb_ref[...].astype(F32)
    tg = y * _silu(z_ref[...].astype(F32))
    ms = jnp.mean(tg * tg, axis=-1, keepdims=True)
    yn = (tg * lax.rsqrt(ms + EPS) * ng_ref[...]).astype(BF16)
    y_ssd = _dot(yn, ws_ref[...])
    y_attn = _dot(attn_ref[...], wa_ref[...])
    y_conv = _dot(u_ref[...], wc_ref[...])
    d = D_MODEL
    mix = (_sigmoid(gl_ref[:, 0:d].astype(F32)) * y_attn
           + _sigmoid(gl_ref[:, d:2 * d].astype(F32)) * y_conv
           + _sigmoid(gl_ref[:, 2 * d:3 * d].astype(F32)) * y_ssd)
    o_ref[...] = x_ref[...] + gate_ref[...] * _dot(mix.astype(BF16), wm_ref[...])


def _merge(x, mod4, layer, seq, latent, proj, attn, u, yf, yb, norm_g, wa, wc, ws, wm):
    t = x.shape[0]
    tm = 256
    tps = seq // tm
    row_fn = (lambda i: LAT_ROW0 + i // tps) if latent else (lambda i: CTX_ROW)
    const = lambda i: (0, 0)
    d = D_MODEL
    return pl.pallas_call(
        _merge_kernel,
        out_shape=jax.ShapeDtypeStruct((t, d), F32),
        grid=(t // tm,),
        in_specs=[
            pl.BlockSpec((tm, d), lambda i: (i, 0)),
            _mod_spec(layer, 2, row_fn),
            pl.BlockSpec((tm, d), lambda i: (i, 0)),
            pl.BlockSpec((tm, d), lambda i: (i, 0)),
            pl.BlockSpec((tm, D_INNER), lambda i: (i, 0)),
            pl.BlockSpec((tm, D_INNER), lambda i: (i, 0)),
            pl.BlockSpec((tm, D_INNER), lambda i: (i, C_Z // D_INNER)),
            pl.BlockSpec((tm, 3 * d), lambda i: (i, C_GATE // (3 * d))),
            pl.BlockSpec((1, D_INNER), const),
            pl.BlockSpec((d, d), const), pl.BlockSpec((d, d), const),
            pl.BlockSpec((D_INNER, d), const), pl.BlockSpec((d, d), const),
        ],
        out_specs=pl.BlockSpec((tm, d), lambda i: (i, 0)),
        compiler_params=_cparams(("parallel",)),
        name="merge",
    )(x, mod4, attn, u, yf, yb, proj, proj, norm_g, wa, wc, ws, wm)


def _ffn1_kernel(x_ref, shift_ref, scale_ref, g_ref, wg_ref, wu_ref, o_ref, h_scr):
    @pl.when(pl.program_id(1) == 0)
    def _():
        h_scr[...] = _norm_mod(x_ref[...], g_ref[...], scale_ref[...], shift_ref[...]).astype(BF16)

    h = h_scr[...]
    o_ref[...] = (_silu(_dot(h, wg_ref[...])) * _dot(h, wu_ref[...])).astype(BF16)


def _ffn1(x, mod4, layer, seq, latent, norm_g, w1):
    t = x.shape[0]
    tm = 1024
    tn = 256
    nj = D_FF // tn
    tps = max(seq // tm, 1)
    row_fn = (lambda i, j: LAT_ROW0 + i // tps) if latent else (lambda i, j: CTX_ROW)
    return pl.pallas_call(
        _ffn1_kernel,
        out_shape=jax.ShapeDtypeStruct((t, D_FF), BF16),
        grid=(t // tm, nj),
        in_specs=[
            pl.BlockSpec((tm, D_MODEL), lambda i, j: (i, 0)),
            _mod_spec(layer, 3, row_fn),
            _mod_spec(layer, 4, row_fn),
            pl.BlockSpec((1, D_MODEL), lambda i, j: (0, 0)),
            pl.BlockSpec((D_MODEL, tn), lambda i, j: (0, j)),
            pl.BlockSpec((D_MODEL, tn), lambda i, j: (0, nj + j)),
        ],
        out_specs=pl.BlockSpec((tm, tn), lambda i, j: (i, j)),
        scratch_shapes=[pltpu.VMEM((tm, D_MODEL), BF16)],
        compiler_params=_cparams(("parallel", "arbitrary")),
        name="ffn_up",
    )(x, mod4, mod4, norm_g, w1, w1)


def _ffn2_kernel(x_ref, gate_ref, a_ref, w_ref, o_ref):
    o_ref[...] = x_ref[...] + gate_ref[...] * _dot(a_ref[...], w_ref[...])


def _ffn2(x, mod4, layer, seq, latent, act, w2):
    t = x.shape[0]
    tm = 512
    tps = max(seq // tm, 1)
    row_fn = (lambda i: LAT_ROW0 + i // tps) if latent else (lambda i: CTX_ROW)
    return pl.pallas_call(
        _ffn2_kernel,
        out_shape=jax.ShapeDtypeStruct((t, D_MODEL), F32),
        grid=(t // tm,),
        in_specs=[
            pl.BlockSpec((tm, D_MODEL), lambda i: (i, 0)),
            _mod_spec(layer, 5, row_fn),
            pl.BlockSpec((tm, D_FF), lambda i: (i, 0)),
            pl.BlockSpec((D_FF, D_MODEL), lambda i: (0, 0)),
        ],
        out_specs=pl.BlockSpec((tm, D_MODEL), lambda i: (i, 0)),
        compiler_params=_cparams(("parallel",)),
        name="ffn_down",
    )(x, mod4, act, w2)


def _rope_tables(n_tok):
    rows = n_tok // GRID_W
    row = jnp.repeat(jnp.arange(rows, dtype=F32), GRID_W)
    col = jnp.tile(jnp.arange(GRID_W, dtype=F32), rows)
    axis_dim = HEAD_DIM // 2
    inv_freq = 1.0 / (ROPE_THETA ** (jnp.arange(0, axis_dim, 2, dtype=F32) / axis_dim))
    ang = jnp.concatenate([row[:, None] * inv_freq, col[:, None] * inv_freq], axis=-1)
    cos = jnp.repeat(jnp.cos(ang), 2, axis=-1)
    sin = jnp.repeat(jnp.sin(ang), 2, axis=-1)
    sign = jnp.tile(jnp.asarray([-1.0, 1.0], F32), HEAD_DIM // 2)
    return cos, sin * sign


def _layer(x, seq, latent, layer, mod4, wts, cache_k4, cache_v4, s0_t, rope_tabs, e3f, e3b):
    proj, dt_raw, *kv = _in_proj(x, mod4, layer, seq, latent, wts["norm1_g"], wts["w_main"], wts["w_dt"],
                                 wts["q_g"], wts["k_g"], rope_tabs)
    attn = _attention(proj, seq, latent, layer, cache_k4, cache_v4)
    u = _gconv(proj, seq, wts["conv_w"])
    xact = _ssd_conv(proj, seq, wts["ssd_conv_w"], wts["ssd_conv_b"])
    dt_raw_t = dt_raw.T
    yf, sf = _ssd(xact, dt_raw, dt_raw_t, seq, False, wts["dt_bias"], wts["a_log"], s0_t, e3f, wts["d_wide"])
    yb, sb = _ssd(xact, dt_raw, dt_raw_t, seq, True, wts["dt_bias"], wts["a_log"], s0_t, e3b, wts["d_wide"])
    x = _merge(x, mod4, layer, seq, latent, proj, attn, u, yf, yb, wts["ssd_norm_g"],
               wts["w_attn_o"], wts["w_conv_o"], wts["w_ssd_o"], wts["w_merge"])
    act = _ffn1(x, mod4, layer, seq, latent, wts["norm2_g"], wts["ffn_w1"])
    x = _ffn2(x, mod4, layer, seq, latent, act, wts["ffn_w2"])
    return x, kv, (sf, sb)


def _layer_weights(l, norm1_g, norm2_g, w_in, q_norm_g, k_norm_g, w_attn_o, conv_w, w_conv_o, ssd_conv_w,
                   ssd_conv_b, ssd_dt_bias, ssd_a_log, ssd_d, ssd_norm_g, w_ssd_o, w_merge, ffn_w1, ffn_w2):
    d = D_MODEL
    kvw = N_KV_HEADS * HEAD_DIM
    o_q, o_k, o_v, o_bg, o_cg, o_cx = 0, d, d + kvw, d + 2 * kvw, 2 * d + 2 * kvw, 3 * d + 2 * kvw
    o_z = o_cx + d
    o_xbc = o_z + D_INNER
    o_dt = o_xbc + 2 * D_INNER
    o_gate = o_dt + 2 * N_SSD_HEADS
    w = w_in[l]
    cols = lambda a, n: w[:, a:a + n]
    w_main = jnp.concatenate([cols(o_z, D_INNER), cols(o_xbc, 2 * D_INNER), cols(o_gate, 3 * d), cols(o_bg, d),
                              cols(o_cg, d), cols(o_cx, d), cols(o_q, d), cols(o_k, kvw), cols(o_v, kvw)],
                             axis=1).astype(BF16)
    w_dt = jnp.concatenate([cols(o_dt, 2 * N_SSD_HEADS), jnp.zeros((d, LANES - 2 * N_SSD_HEADS), F32)],
                           axis=1).astype(BF16)
    return {
        "norm1_g": norm1_g[l].reshape(1, d), "norm2_g": norm2_g[l].reshape(1, d),
        "w_main": w_main, "w_dt": w_dt,
        "q_g": q_norm_g[l].reshape(1, HEAD_DIM), "k_g": k_norm_g[l].reshape(1, HEAD_DIM),
        "w_attn_o": w_attn_o[l].astype(BF16), "conv_w": conv_w[l], "w_conv_o": w_conv_o[l].astype(BF16),
        "ssd_conv_w": ssd_conv_w[l], "ssd_conv_b": ssd_conv_b[l].reshape(1, -1),
        "dt_bias": ssd_dt_bias[l], "a_log": ssd_a_log[l],
        "d_wide": jnp.repeat(ssd_d[l], SSD_HEADDIM).reshape(1, D_INNER),
        "ssd_norm_g": ssd_norm_g[l].reshape(1, D_INNER), "w_ssd_o": w_ssd_o[l].astype(BF16),
        "w_merge": w_merge[l].astype(BF16), "ffn_w1": ffn_w1[l].astype(BF16), "ffn_w2": ffn_w2[l].astype(BF16),
    }


def _states_to_lanes(s):
    lead = s.shape[:-3]
    return jnp.moveaxis(s, -1, -3).reshape(lead + (D_STATE, D_INNER))


def _states_from_lanes(s):
    lead = s.shape[:-2]
    return jnp.moveaxis(s.reshape(lead + (D_STATE, N_SSD_HEADS, SSD_HEADDIM)), -3, -1)


def kernel(x_prompt, x_sample, c, cache_k, cache_v, state_ssd, c_ctx, ada_w, ada_b, norm1_g, norm2_g, w_in,
           q_norm_g, k_norm_g, w_attn_o, conv_w, w_conv_o, ssd_conv_w, ssd_conv_b, ssd_dt_bias, ssd_a_log, ssd_d,
           ssd_norm_g, w_ssd_o, w_merge, ffn_w1, ffn_w2):
    nb_ctx, seq_ctx, d = x_prompt.shape
    nb_lat, seq_lat, _ = x_sample.shape
    past = cache_k.shape[2]
    cond = jnp.concatenate([c_ctx[None, :], c, jnp.zeros((MOD_ROWS - nb_lat - 1, d), F32)], axis=0)
    mod4 = _modulation(cond, ada_w, ada_b).reshape(DEPTH, MOD_ROWS, 1, 6 * d)
    cache_k4 = cache_k.reshape(nb_lat, DEPTH, past, N_KV_HEADS * HEAD_DIM)
    cache_v4 = cache_v.reshape(nb_lat, DEPTH, past, N_KV_HEADS * HEAD_DIM)
    rope_tabs = _rope_tables(seq_lat)
    e3f, e3b = _expansion_matrix(False), _expansion_matrix(True)
    zero_state = jnp.zeros((nb_ctx, 2, D_STATE, D_INNER), F32)

    y_ctx = x_prompt.reshape(nb_ctx * seq_ctx, d)
    y_lat = x_sample.reshape(nb_lat * seq_lat, d)
    ks, vs, ss = [], [], []
    for l in range(DEPTH):
        wts = _layer_weights(l, norm1_g, norm2_g, w_in, q_norm_g, k_norm_g, w_attn_o, conv_w, w_conv_o,
                             ssd_conv_w, ssd_conv_b, ssd_dt_bias, ssd_a_log, ssd_d, ssd_norm_g, w_ssd_o,
                             w_merge, ffn_w1, ffn_w2)
        y_ctx, (k32, v32), (sf, sb) = _layer(y_ctx, seq_ctx, False, l, mod4, wts, None, None, zero_state,
                                             None, e3f, e3b)
        y_lat, _, _ = _layer(y_lat, seq_lat, True, l, mod4, wts, cache_k4, cache_v4,
                             _states_to_lanes(state_ssd[:, l]), rope_tabs, e3f, e3b)
        ks.append(k32.reshape(nb_ctx, seq_ctx, N_KV_HEADS, HEAD_DIM))
        vs.append(v32.reshape(nb_ctx, seq_ctx, N_KV_HEADS, HEAD_DIM))
        ss.append(_states_from_lanes(jnp.stack([sf, sb], axis=1)))
    return (y_ctx.reshape(nb_ctx, seq_ctx, d), y_lat.reshape(nb_lat, seq_lat, d),
            jnp.stack(ks, axis=1), jnp.stack(vs, axis=1), jnp.stack(ss, axis=1))
```

```python
import functools

import jax
import jax.numpy as jnp
from jax import lax
from jax.experimental import pallas as pl
from jax.experimental.pallas import tpu as pltpu

F32 = jnp.float32
BF16 = jnp.bfloat16

D_MODEL = 1024
DEPTH = 2
GRID_W = 64
N_HEADS = 8
N_KV_HEADS = 2
HEAD_DIM = 128
KV_REP = N_HEADS // N_KV_HEADS
ROPE_THETA = 10000.0
CONV_K = 3
D_INNER = 2 * D_MODEL
SSD_HEADDIM = 64
N_SSD_HEADS = D_INNER // SSD_HEADDIM
N_SSD_GROUPS = 8
SSD_HPG = N_SSD_HEADS // N_SSD_GROUPS
D_STATE = 128
CHUNK = 128
D_FF = -(-8 * D_MODEL // 768) * 256
EPS = 1e-6

LANES = 128
BF16_ROWS = 16
VMEM_LIMIT = 56 * 1024 * 1024

XBC_W = 2 * D_INNER
C_XBC = 0
C_Z = C_XBC + XBC_W
C_GATE = C_Z + D_INNER
C_BG = C_GATE + 3 * D_MODEL
C_CG = C_BG + D_MODEL
C_CX = C_CG + D_MODEL
C_Q = C_CX + D_MODEL
C_K = C_Q + D_MODEL
C_V = C_K + N_KV_HEADS * HEAD_DIM
N_MAIN = C_V + N_KV_HEADS * HEAD_DIM
PROJ_TN = N_MAIN - C_Q
J_QKV = C_Q // PROJ_TN
LOG2E = 1.4426950408889634
MOD_ROWS = 16
CTX_ROW = 0
LAT_ROW0 = 1


def _cparams(sem):
    return pltpu.CompilerParams(dimension_semantics=sem, vmem_limit_bytes=VMEM_LIMIT)


def _dot(a, b):
    return jnp.dot(a, b, preferred_element_type=F32)


def _sigmoid(x):
    return 1.0 / (1.0 + jnp.exp(-x))


def _silu(x):
    return x * _sigmoid(x)


def _softplus(x):
    return jnp.maximum(x, 0.0) + jnp.log1p(jnp.exp(-jnp.abs(x)))


def _split3(x):
    hi = x.astype(BF16)
    r1 = x - hi.astype(F32)
    mid = r1.astype(BF16)
    lo = (r1 - mid.astype(F32)).astype(BF16)
    return hi, mid, lo


def _mod_kernel(c_ref, w_ref, b_ref, o_ref):
    s = _silu(c_ref[...]).astype(BF16)
    o_ref[...] = _dot(s, w_ref[...].astype(BF16)) + b_ref[...]


def _modulation(cond, ada_w, ada_b):
    tn = 1536
    return pl.pallas_call(
        _mod_kernel,
        out_shape=jax.ShapeDtypeStruct((DEPTH, MOD_ROWS, 6 * D_MODEL), F32),
        grid=(DEPTH, 6 * D_MODEL // tn),
        in_specs=[
            pl.BlockSpec((MOD_ROWS, D_MODEL), lambda l, j: (0, 0)),
            pl.BlockSpec((None, D_MODEL, tn), lambda l, j: (l, 0, j)),
            pl.BlockSpec((None, 1, tn), lambda l, j: (l, 0, j)),
        ],
        out_specs=pl.BlockSpec((None, MOD_ROWS, tn), lambda l, j: (l, 0, j)),
        compiler_params=_cparams(("parallel", "parallel")),
        name="modulation",
    )(cond, ada_w, ada_b.reshape(DEPTH, 1, 6 * D_MODEL))


def _mod_spec(layer, which, row_fn):
    return pl.BlockSpec((None, None, 1, D_MODEL), lambda *ids: (layer, row_fn(*ids), 0, which))


def _norm_mod(x, g, scale, shift):
    ms = jnp.mean(x * x, axis=-1, keepdims=True)
    return (x * lax.rsqrt(ms + EPS) * g) * (1.0 + scale) + shift


def _head_norm(a, g):
    ms = jnp.mean(a * a, axis=-1, keepdims=True)
    return a * lax.rsqrt(ms + EPS) * g


def _rope(a, cos, sin_signed):
    lane = lax.broadcasted_iota(jnp.int32, a.shape, 1)
    nxt = pltpu.roll(a, HEAD_DIM - 1, 1)
    prv = pltpu.roll(a, 1, 1)
    swapped = jnp.where(lane % 2 == 0, nxt, prv)
    return a * cos + swapped * sin_signed


def _inproj_kernel(latent, x_ref, shift_ref, scale_ref, g_ref, w_ref, wdt_ref, qg_ref, kg_ref, *rest):
    if latent:
        cos_ref, sin_ref, proj_ref, dt_ref, h_scr = rest
    else:
        proj_ref, dt_ref, k32_ref, v32_ref, h_scr = rest
    j = pl.program_id(1)

    @pl.when(j == 0)
    def _():
        h = _norm_mod(x_ref[...], g_ref[...], scale_ref[...], shift_ref[...]).astype(BF16)
        h_scr[...] = h
        dt_ref[...] = _dot(h, wdt_ref[...])

    acc = _dot(h_scr[...], w_ref[...])

    @pl.when(j != J_QKV)
    def _():
        proj_ref[...] = acc.astype(BF16)

    @pl.when(j == J_QKV)
    def _():
        q_scale = HEAD_DIM ** -0.5 * LOG2E
        for hh in range(N_HEADS):
            sl = slice(hh * HEAD_DIM, (hh + 1) * HEAD_DIM)
            q = _head_norm(acc[:, sl], qg_ref[...])
            if latent:
                q = _rope(q, cos_ref[...], sin_ref[...])
            proj_ref[:, sl] = (q * q_scale).astype(BF16)
        k0 = C_K - C_Q
        for hh in range(N_KV_HEADS):
            sl = slice(k0 + hh * HEAD_DIM, k0 + (hh + 1) * HEAD_DIM)
            k = _head_norm(acc[:, sl], kg_ref[...])
            if latent:
                k = _rope(k, cos_ref[...], sin_ref[...])
            else:
                k32_ref[:, hh * HEAD_DIM:(hh + 1) * HEAD_DIM] = k
            proj_ref[:, sl] = k.astype(BF16)
        v0 = C_V - C_Q
        v = acc[:, v0:]
        if not latent:
            v32_ref[...] = v
        proj_ref[:, v0:] = v.astype(BF16)


def _in_proj(x, mod4, layer, seq, latent, norm_g, w_main, w_dt, q_g, k_g, rope_tabs):
    t = x.shape[0]
    tm = 1024
    tiles_per_seq = max(seq // tm, 1)
    row_fn = (lambda i, j: LAT_ROW0 + i // tiles_per_seq) if latent else (lambda i, j: CTX_ROW)
    in_specs = [
        pl.BlockSpec((tm, D_MODEL), lambda i, j: (i, 0)),
        _mod_spec(layer, 0, row_fn),
        _mod_spec(layer, 1, row_fn),
        pl.BlockSpec((1, D_MODEL), lambda i, j: (0, 0)),
        pl.BlockSpec((D_MODEL, PROJ_TN), lambda i, j: (0, j)),
        pl.BlockSpec((D_MODEL, LANES), lambda i, j: (0, 0)),
        pl.BlockSpec((1, HEAD_DIM), lambda i, j: (0, 0)),
        pl.BlockSpec((1, HEAD_DIM), lambda i, j: (0, 0)),
    ]
    args = [x, mod4, mod4, norm_g, w_main, w_dt, q_g, k_g]
    out_shape = [jax.ShapeDtypeStruct((t, N_MAIN), BF16), jax.ShapeDtypeStruct((t, LANES), F32)]
    out_specs = [pl.BlockSpec((tm, PROJ_TN), lambda i, j: (i, j)), pl.BlockSpec((tm, LANES), lambda i, j: (i, 0))]
    if latent:
        tab_spec = pl.BlockSpec((tm, HEAD_DIM), lambda i, j: (i % tiles_per_seq, 0))
        in_specs += [tab_spec, tab_spec]
        args += list(rope_tabs)
    else:
        kv_w = N_KV_HEADS * HEAD_DIM
        out_shape += [jax.ShapeDtypeStruct((t, kv_w), F32)] * 2
        out_specs += [pl.BlockSpec((tm, kv_w), lambda i, j: (i, 0))] * 2
    return pl.pallas_call(
        functools.partial(_inproj_kernel, latent),
        out_shape=out_shape,
        grid=(t // tm, N_MAIN // PROJ_TN),
        in_specs=in_specs,
        out_specs=out_specs,
        scratch_shapes=[pltpu.VMEM((tm, D_MODEL), BF16)],
        compiler_params=_cparams(("parallel", "arbitrary")),
        name="in_proj_lat" if latent else "in_proj_ctx",
    )(*args)


def _attn_kernel(latent, q_ref, k_ref, v_ref, *rest):
    if latent:
        ck_ref, cv_ref, o_ref = rest
    else:
        (o_ref,) = rest
    nt = (((1,), (1,)), ((), ()))

    def with_ones(v):
        return jnp.concatenate([v, jnp.ones_like(v)], axis=1)

    k = k_ref[...]
    v1 = with_ones(v_ref[...])
    if latent:
        ck = ck_ref[...].astype(BF16)
        cv1 = with_ones(cv_ref[...].astype(BF16))
    for hh in range(KV_REP):
        sl = slice(hh * HEAD_DIM, (hh + 1) * HEAD_DIM)
        q = q_ref[:, sl]
        s = lax.dot_general(q, k, nt, preferred_element_type=F32)
        m = jnp.max(s, axis=-1, keepdims=True)
        if latent:
            s2 = lax.dot_general(q, ck, nt, preferred_element_type=F32)
            m = jnp.maximum(m, jnp.max(s2, axis=-1, keepdims=True))
        o = _dot(jnp.exp2(s - m).astype(BF16), v1)
        if latent:
            o = o + _dot(jnp.exp2(s2 - m).astype(BF16), cv1)
        o_ref[:, sl] = (o[:, :HEAD_DIM] / o[:, HEAD_DIM:]).astype(BF16)


def _attention(proj, seq, latent, layer, cache_k4, cache_v4):
    t = proj.shape[0]
    nb = t // seq
    tq = 256
    qt = seq // tq
    gw = KV_REP * HEAD_DIM
    in_specs = [
        pl.BlockSpec((tq, gw), lambda b, g, i: (b * qt + i, C_Q // gw + g)),
        pl.BlockSpec((seq, HEAD_DIM), lambda b, g, i: (b, C_K // HEAD_DIM + g)),
        pl.BlockSpec((seq, HEAD_DIM), lambda b, g, i: (b, C_V // HEAD_DIM + g)),
    ]
    args = [proj, proj, proj]
    if latent:
        past = cache_k4.shape[2]
        cspec = pl.BlockSpec((None, None, past, HEAD_DIM), lambda b, g, i: (b, layer, 0, g))
        in_specs += [cspec, cspec]
        args += [cache_k4, cache_v4]
    return pl.pallas_call(
        functools.partial(_attn_kernel, latent),
        out_shape=jax.ShapeDtypeStruct((t, N_HEADS * HEAD_DIM), BF16),
        grid=(nb, N_KV_HEADS, qt),
        in_specs=in_specs,
        out_specs=pl.BlockSpec((tq, gw), lambda b, g, i: (b * qt + i, g)),
        compiler_params=_cparams(("parallel", "parallel", "arbitrary")),
        name="attention_lat" if latent else "attention_ctx",
    )(*args)


def _conv3(p, prev_row, next_row, w, first, last):
    tm = p.shape[0]
    rid = lax.broadcasted_iota(jnp.int32, p.shape, 0)
    prev_row = jnp.where(first, 0.0, prev_row)
    next_row = jnp.where(last, 0.0, next_row)
    dn = jnp.where(rid == 0, prev_row, pltpu.roll(p, 1, 0))
    up = jnp.where(rid == tm - 1, next_row, pltpu.roll(p, tm - 1, 0))
    return w[0:1, :] * dn + w[1:2, :] * p + w[2:3, :] * up


def _gconv_kernel(tm, seq, bg_ref, cg_ref, cx_ref, cgp_ref, cxp_ref, cgn_ref, cxn_ref, w_ref, o_ref):
    i = pl.program_id(0)
    tps = seq // tm
    p = cg_ref[...].astype(F32) * cx_ref[...].astype(F32)
    r = BF16_ROWS - 1
    prev_row = cgp_ref[r:r + 1, :].astype(F32) * cxp_ref[r:r + 1, :].astype(F32)
    next_row = cgn_ref[0:1, :].astype(F32) * cxn_ref[0:1, :].astype(F32)
    conv = _conv3(p, prev_row, next_row, w_ref[...], (i % tps) == 0, (i % tps) == tps - 1)
    o_ref[...] = (bg_ref[...].astype(F32) * conv).astype(BF16)


def _gconv(proj, seq, conv_w):
    t = proj.shape[0]
    tm = min(512, seq)
    c = D_MODEL
    rpt = tm // BF16_ROWS
    last_blk = t // BF16_ROWS - 1

    def main(cb):
        return pl.BlockSpec((tm, c), lambda i: (i, cb))

    def prev(cb):
        return pl.BlockSpec((BF16_ROWS, c), lambda i: (jnp.maximum(i * rpt - 1, 0), cb))

    def nxt(cb):
        return pl.BlockSpec((BF16_ROWS, c), lambda i: (jnp.minimum((i + 1) * rpt, last_blk), cb))

    return pl.pallas_call(
        functools.partial(_gconv_kernel, tm, seq),
        out_shape=jax.ShapeDtypeStruct((t, c), BF16),
        grid=(t // tm,),
        in_specs=[main(C_BG // c), main(C_CG // c), main(C_CX // c), prev(C_CG // c), prev(C_CX // c),
                  nxt(C_CG // c), nxt(C_CX // c), pl.BlockSpec((CONV_K, c), lambda i: (0, 0))],
        out_specs=pl.BlockSpec((tm, c), lambda i: (i, 0)),
        compiler_params=_cparams(("parallel",)),
        name="gated_conv",
    )(proj, proj, proj, proj, proj, proj, proj, conv_w)


GW = SSD_HPG * SSD_HEADDIM
NBC = N_SSD_GROUPS * D_STATE
CONV_SLAB = 1024


def _ssd_chunk(rev, xcols, dtr_ref, dtrt_ref, brow_ref, bcol_ref, arow_ref, acol_ref, st_scr, d_ref, emit):
    ii = lax.broadcasted_iota(jnp.int32, (CHUNK, CHUNK), 0)
    jj = lax.broadcasted_iota(jnp.int32, (CHUNK, CHUNK), 1)
    keep = (ii <= jj) if rev else (ii >= jj)
    tri = jnp.where(keep, 1.0, 0.0).astype(BF16)
    tri_t = jnp.where((jj <= ii) if rev else (jj >= ii), 1.0, 0.0).astype(BF16)

    dt = _softplus(dtr_ref[...] + brow_ref[...])
    acs = sum(_dot(tri, part) for part in _split3(dt * (-jnp.exp(arow_ref[...]))))
    dt_t = _softplus(dtrt_ref[...] + bcol_ref[...])
    acs_t = sum(_dot(part, tri_t) for part in _split3(dt_t * (-jnp.exp(acol_ref[...]))))

    edge = 0 if rev else CHUNK - 1
    ea = jnp.exp(acs)
    dec_row = ea[edge:edge + 1, :]
    w_t = dt_t * jnp.exp(acs_t[:, edge:edge + 1] - acs_t)
    acs2 = acs * LOG2E
    l2_t = acs_t * LOG2E - jnp.log2(dt_t)
    head_of_lane = lax.broadcasted_iota(jnp.int32, (CHUNK, GW), 1) // SSD_HEADDIM
    head_of_lane_row = lax.broadcasted_iota(jnp.int32, (1, GW), 1) // SSD_HEADDIM
    lane0 = N_SSD_HEADS if rev else 0
    zero_blk = jnp.zeros((D_STATE, CHUNK), BF16)

    for g in range(N_SSD_GROUPS):
        gs = slice(g * GW, (g + 1) * GW)
        xs = xcols(g * GW, (g + 1) * GW)
        bc = xcols(D_INNER + g * D_STATE, D_INNER + (g + 1) * D_STATE)
        cc = xcols(D_INNER + NBC + g * D_STATE, D_INNER + NBC + (g + 1) * D_STATE)
        bc_t = bc.astype(F32).T
        cc_f = cc.astype(F32)
        cb = _dot(cc, bc_t.astype(BF16))
        st = st_scr[:, gs]
        st_b = st.astype(BF16)
        acc = None
        dec_w = jnp.zeros((1, GW), F32)
        for r in range(SSD_HPG):
            ln = lane0 + g * SSD_HPG + r
            seg = acs2[:, ln:ln + 1] - l2_t[ln:ln + 1, :]
            m = cb * jnp.exp2(jnp.where(keep, seg, -jnp.inf))
            cce = cc_f * ea[:, ln:ln + 1]
            bw = bc_t * w_t[ln:ln + 1, :]
            lhs = jnp.concatenate([jnp.concatenate([m.astype(BF16), cce.astype(BF16)], axis=1),
                                   jnp.concatenate([bw.astype(BF16), zero_blk], axis=1)], axis=0)
            sel = head_of_lane == r
            rhs = jnp.concatenate([jnp.where(sel, xs, jnp.zeros_like(xs)),
                                   jnp.where(sel, st_b, jnp.zeros_like(st_b))], axis=0)
            part = _dot(lhs, rhs)
            acc = part if acc is None else acc + part
            dec_w = jnp.where(head_of_lane_row == r, dec_row[:, ln:ln + 1], dec_w)
        y = acc[:CHUNK, :]
        if not rev:
            y = y + d_ref[:, gs] * xs.astype(F32)
        emit(g, y)
        st_scr[:, gs] = st * dec_w + acc[CHUNK:, :]


def _ssd_kernel(nchunks, xbc_ref, xp_ref, xn_ref, cw_ref, cb_ref, dtr_ref, dtrt_ref, brow_ref, bcol_ref,
                arow_ref, acol_ref, s0_ref, d_ref, y_ref, sfin_ref, xact_scr, yf_scr, st_scr):
    s = pl.program_id(1)
    scan_args = (dtr_ref, dtrt_ref, brow_ref, bcol_ref, arow_ref, acol_ref, st_scr, d_ref)

    @pl.when(s == 0)
    def _():
        st_scr[...] = s0_ref[0]

    @pl.when(s < nchunks)
    def _():
        c = s
        r = BF16_ROWS - 1
        for k in range(XBC_W // CONV_SLAB):
            sl = slice(k * CONV_SLAB, (k + 1) * CONV_SLAB)
            conv = _conv3(xbc_ref[:, sl].astype(F32), xp_ref[r:r + 1, sl].astype(F32),
                          xn_ref[0:1, sl].astype(F32), cw_ref[:, sl], c == 0, c == nchunks - 1)
            xact_scr[c, :, sl] = _silu(conv + cb_ref[:, sl]).astype(BF16)

        def emit(g, y):
            yf_scr[c, :, g * GW:(g + 1) * GW] = y.astype(BF16)

        _ssd_chunk(False, lambda lo, hi: xact_scr[c, :, lo:hi], *scan_args, emit)

    @pl.when(s == nchunks - 1)
    def _():
        sfin_ref[0] = st_scr[...]

    @pl.when(s == nchunks)
    def _():
        st_scr[...] = s0_ref[1]

    @pl.when(s >= nchunks)
    def _():
        c = 2 * nchunks - 1 - s

        def emit(g, y):
            gs = slice(g * GW, (g + 1) * GW)
            y_ref[:, gs] = (y + yf_scr[c, :, gs].astype(F32)).astype(BF16)

        _ssd_chunk(True, lambda lo, hi: xact_scr[c, :, lo:hi], *scan_args, emit)

    @pl.when(s == 2 * nchunks - 1)
    def _():
        sfin_ref[1] = st_scr[...]


def _ssd(proj, dt_raw, dt_raw_t, seq, conv_w, conv_b, bias, a_log, s0_t, d_wide):
    t = proj.shape[0]
    nb = t // seq
    nchunks = seq // CHUNK
    rpc = CHUNK // BF16_ROWS
    last_blk = t // BF16_ROWS - 1
    xb = C_XBC // XBC_W

    def chunk_in(b, s):
        return b * nchunks + jnp.minimum(s, nchunks - 1)

    def chunk_of(b, s):
        return b * nchunks + jnp.where(s < nchunks, s, 2 * nchunks - 1 - s)

    def chunk_out(b, s):
        return b * nchunks + jnp.where(s < nchunks, nchunks - 1, 2 * nchunks - 1 - s)

    pad = jnp.zeros((LANES - 2 * N_SSD_HEADS,), F32)
    b_row = jnp.concatenate([bias.reshape(-1), pad]).reshape(1, LANES)
    a_row = jnp.concatenate([a_log.reshape(-1), pad]).reshape(1, LANES)
    const = lambda b, s: (0, 0)
    state_spec = pl.BlockSpec((None, 2, D_STATE, D_INNER), lambda b, s: (b, 0, 0, 0))
    return pl.pallas_call(
        functools.partial(_ssd_kernel, nchunks),
        out_shape=[jax.ShapeDtypeStruct((t, D_INNER), BF16), jax.ShapeDtypeStruct((nb, 2, D_STATE, D_INNER), F32)],
        grid=(nb, 2 * nchunks),
        in_specs=[
            pl.BlockSpec((CHUNK, XBC_W), lambda b, s: (chunk_in(b, s), xb)),
            pl.BlockSpec((BF16_ROWS, XBC_W), lambda b, s: (jnp.maximum(chunk_in(b, s) * rpc - 1, 0), xb)),
            pl.BlockSpec((BF16_ROWS, XBC_W), lambda b, s: (jnp.minimum((chunk_in(b, s) + 1) * rpc, last_blk), xb)),
            pl.BlockSpec((CONV_K, XBC_W), const),
            pl.BlockSpec((1, XBC_W), const),
            pl.BlockSpec((CHUNK, LANES), lambda b, s: (chunk_of(b, s), 0)),
            pl.BlockSpec((LANES, CHUNK), lambda b, s: (0, chunk_of(b, s))),
            pl.BlockSpec((1, LANES), const), pl.BlockSpec((LANES, 1), const),
            pl.BlockSpec((1, LANES), const), pl.BlockSpec((LANES, 1), const),
            state_spec,
            pl.BlockSpec((1, D_INNER), const),
        ],
        out_specs=[pl.BlockSpec((CHUNK, D_INNER), lambda b, s: (chunk_out(b, s), 0)), state_spec],
        scratch_shapes=[pltpu.VMEM((nchunks, CHUNK, XBC_W), BF16), pltpu.VMEM((nchunks, CHUNK, D_INNER), BF16),
                        pltpu.VMEM((D_STATE, D_INNER), F32)],
        compiler_params=_cparams(("parallel", "arbitrary")),
        name="ssd",
    )(proj, proj, proj, conv_w, conv_b, dt_raw, dt_raw_t, b_row, b_row.reshape(LANES, 1), a_row,
      a_row.reshape(LANES, 1), s0_t, d_wide)


def _merge_kernel(x_ref, gate_ref, attn_ref, u_ref, y_ref, z_ref, gl_ref, ng_ref,
                  wa_ref, wc_ref, ws_ref, wm_ref, o_ref):
    tg = y_ref[...].astype(F32) * _silu(z_ref[...].astype(F32))
    ms = jnp.mean(tg * tg, axis=-1, keepdims=True)
    yn = (tg * lax.rsqrt(ms + EPS) * ng_ref[...]).astype(BF16)
    y_ssd = _dot(yn, ws_ref[...])
    y_attn = _dot(attn_ref[...], wa_ref[...])
    y_conv = _dot(u_ref[...], wc_ref[...])
    d = D_MODEL
    mix = (_sigmoid(gl_ref[:, 0:d].astype(F32)) * y_attn
           + _sigmoid(gl_ref[:, d:2 * d].astype(F32)) * y_conv
           + _sigmoid(gl_ref[:, 2 * d:3 * d].astype(F32)) * y_ssd)
    o_ref[...] = x_ref[...] + gate_ref[...] * _dot(mix.astype(BF16), wm_ref[...])


def _merge(x, mod4, layer, seq, latent, proj, attn, u, y, norm_g, wa, wc, ws, wm):
    t = x.shape[0]
    tm = 256
    tps = seq // tm
    row_fn = (lambda i: LAT_ROW0 + i // tps) if latent else (lambda i: CTX_ROW)
    const = lambda i: (0, 0)
    d = D_MODEL
    return pl.pallas_call(
        _merge_kernel,
        out_shape=jax.ShapeDtypeStruct((t, d), F32),
        grid=(t // tm,),
        in_specs=[
            pl.BlockSpec((tm, d), lambda i: (i, 0)),
            _mod_spec(layer, 2, row_fn),
            pl.BlockSpec((tm, d), lambda i: (i, 0)),
            pl.BlockSpec((tm, d), lambda i: (i, 0)),
            pl.BlockSpec((tm, D_INNER), lambda i: (i, 0)),
            pl.BlockSpec((tm, D_INNER), lambda i: (i, C_Z // D_INNER)),
            pl.BlockSpec((tm, 3 * d), lambda i: (i, C_GATE // (3 * d))),
            pl.BlockSpec((1, D_INNER), const),
            pl.BlockSpec((d, d), const), pl.BlockSpec((d, d), const),
            pl.BlockSpec((D_INNER, d), const), pl.BlockSpec((d, d), const),
        ],
        out_specs=pl.BlockSpec((tm, d), lambda i: (i, 0)),
        compiler_params=_cparams(("parallel",)),
        name="merge",
    )(x, mod4, attn, u, y, proj, proj, norm_g, wa, wc, ws, wm)


def _ffn1_kernel(x_ref, shift_ref, scale_ref, g_ref, wg_ref, wu_ref, o_ref, h_scr):
    @pl.when(pl.program_id(1) == 0)
    def _():
        h_scr[...] = _norm_mod(x_ref[...], g_ref[...], scale_ref[...], shift_ref[...]).astype(BF16)

    h = h_scr[...]
    o_ref[...] = (_silu(_dot(h, wg_ref[...])) * _dot(h, wu_ref[...])).astype(BF16)


def _ffn1(x, mod4, layer, seq, latent, norm_g, w1):
    t = x.shape[0]
    tm = 1024
    tn = D_FF // 2
    nj = D_FF // tn
    tps = max(seq // tm, 1)
    row_fn = (lambda i, j: LAT_ROW0 + i // tps) if latent else (lambda i, j: CTX_ROW)
    return pl.pallas_call(
        _ffn1_kernel,
        out_shape=jax.ShapeDtypeStruct((t, D_FF), BF16),
        grid=(t // tm, nj),
        in_specs=[
            pl.BlockSpec((tm, D_MODEL), lambda i, j: (i, 0)),
            _mod_spec(layer, 3, row_fn),
            _mod_spec(layer, 4, row_fn),
            pl.BlockSpec((1, D_MODEL), lambda i, j: (0, 0)),
            pl.BlockSpec((D_MODEL, tn), lambda i, j: (0, j)),
            pl.BlockSpec((D_MODEL, tn), lambda i, j: (0, nj + j)),
        ],
        out_specs=pl.BlockSpec((tm, tn), lambda i, j: (i, j)),
        scratch_shapes=[pltpu.VMEM((tm, D_MODEL), BF16)],
        compiler_params=_cparams(("parallel", "arbitrary")),
        name="ffn_up",
    )(x, mod4, mod4, norm_g, w1, w1)


def _ffn2_kernel(x_ref, gate_ref, a_ref, w_ref, o_ref):
    o_ref[...] = x_ref[...] + gate_ref[...] * _dot(a_ref[...], w_ref[...])


def _ffn2(x, mod4, layer, seq, latent, act, w2):
    t = x.shape[0]
    tm = 512
    tps = max(seq // tm, 1)
    row_fn = (lambda i: LAT_ROW0 + i // tps) if latent else (lambda i: CTX_ROW)
    return pl.pallas_call(
        _ffn2_kernel,
        out_shape=jax.ShapeDtypeStruct((t, D_MODEL), F32),
        grid=(t // tm,),
        in_specs=[
            pl.BlockSpec((tm, D_MODEL), lambda i: (i, 0)),
            _mod_spec(layer, 5, row_fn),
            pl.BlockSpec((tm, D_FF), lambda i: (i, 0)),
            pl.BlockSpec((D_FF, D_MODEL), lambda i: (0, 0)),
        ],
        out_specs=pl.BlockSpec((tm, D_MODEL), lambda i: (i, 0)),
        compiler_params=_cparams(("parallel",)),
        name="ffn_down",
    )(x, mod4, act, w2)


def _rope_tables(n_tok):
    rows = n_tok // GRID_W
    row = jnp.repeat(jnp.arange(rows, dtype=F32), GRID_W)
    col = jnp.tile(jnp.arange(GRID_W, dtype=F32), rows)
    axis_dim = HEAD_DIM // 2
    inv_freq = 1.0 / (ROPE_THETA ** (jnp.arange(0, axis_dim, 2, dtype=F32) / axis_dim))
    ang = jnp.concatenate([row[:, None] * inv_freq, col[:, None] * inv_freq], axis=-1)
    cos = jnp.repeat(jnp.cos(ang), 2, axis=-1)
    sin = jnp.repeat(jnp.sin(ang), 2, axis=-1)
    sign = jnp.tile(jnp.asarray([-1.0, 1.0], F32), HEAD_DIM // 2)
    return cos, sin * sign


def _layer(x, seq, latent, layer, mod4, wts, cache_k4, cache_v4, s0_t, rope_tabs):
    proj, dt_raw, *kv = _in_proj(x, mod4, layer, seq, latent, wts["norm1_g"], wts["w_main"], wts["w_dt"],
                                 wts["q_g"], wts["k_g"], rope_tabs)
    attn = _attention(proj, seq, latent, layer, cache_k4, cache_v4)
    u = _gconv(proj, seq, wts["conv_w"])
    y, s_fin = _ssd(proj, dt_raw, dt_raw.T, seq, wts["ssd_conv_w"], wts["ssd_conv_b"], wts["dt_bias"],
                    wts["a_log"], s0_t, wts["d_wide"])
    x = _merge(x, mod4, layer, seq, latent, proj, attn, u, y, wts["ssd_norm_g"],
               wts["w_attn_o"], wts["w_conv_o"], wts["w_ssd_o"], wts["w_merge"])
    act = _ffn1(x, mod4, layer, seq, latent, wts["norm2_g"], wts["ffn_w1"])
    x = _ffn2(x, mod4, layer, seq, latent, act, wts["ffn_w2"])
    return x, kv, s_fin


def _layer_weights(l, norm1_g, norm2_g, w_in, q_norm_g, k_norm_g, w_attn_o, conv_w, w_conv_o, ssd_conv_w,
                   ssd_conv_b, ssd_dt_bias, ssd_a_log, ssd_d, ssd_norm_g, w_ssd_o, w_merge, ffn_w1, ffn_w2):
    d = D_MODEL
    kvw = N_KV_HEADS * HEAD_DIM
    o_q, o_k, o_v, o_bg, o_cg, o_cx = 0, d, d + kvw, d + 2 * kvw, 2 * d + 2 * kvw, 3 * d + 2 * kvw
    o_z = o_cx + d
    o_xbc = o_z + D_INNER
    o_dt = o_xbc + XBC_W
    o_gate = o_dt + 2 * N_SSD_HEADS
    w = w_in[l]
    cols = lambda a, n: w[:, a:a + n]
    w_main = jnp.concatenate([cols(o_xbc, XBC_W), cols(o_z, D_INNER), cols(o_gate, 3 * d), cols(o_bg, d),
                              cols(o_cg, d), cols(o_cx, d), cols(o_q, d), cols(o_k, kvw), cols(o_v, kvw)],
                             axis=1).astype(BF16)
    w_dt = jnp.concatenate([cols(o_dt, 2 * N_SSD_HEADS), jnp.zeros((d, LANES - 2 * N_SSD_HEADS), F32)],
                           axis=1).astype(BF16)
    return {
        "norm1_g": norm1_g[l].reshape(1, d), "norm2_g": norm2_g[l].reshape(1, d),
        "w_main": w_main, "w_dt": w_dt,
        "q_g": q_norm_g[l].reshape(1, HEAD_DIM), "k_g": k_norm_g[l].reshape(1, HEAD_DIM),
        "w_attn_o": w_attn_o[l].astype(BF16), "conv_w": conv_w[l], "w_conv_o": w_conv_o[l].astype(BF16),
        "ssd_conv_w": ssd_conv_w[l], "ssd_conv_b": ssd_conv_b[l].reshape(1, -1),
        "dt_bias": ssd_dt_bias[l], "a_log": ssd_a_log[l],
        "d_wide": jnp.repeat(ssd_d[l], SSD_HEADDIM).reshape(1, D_INNER),
        "ssd_norm_g": ssd_norm_g[l].reshape(1, D_INNER), "w_ssd_o": w_ssd_o[l].astype(BF16),
        "w_merge": w_merge[l].astype(BF16), "ffn_w1": ffn_w1[l].astype(BF16), "ffn_w2": ffn_w2[l].astype(BF16),
    }


def _states_to_lanes(s):
    lead = s.shape[:-3]
    return jnp.moveaxis(s, -1, -3).reshape(lead + (D_STATE, D_INNER))


def _states_from_lanes(s):
    lead = s.shape[:-2]
    return jnp.moveaxis(s.reshape(lead + (D_STATE, N_SSD_HEADS, SSD_HEADDIM)), -3, -1)


def kernel(x_prompt, x_sample, c, cache_k, cache_v, state_ssd, c_ctx, ada_w, ada_b, norm1_g, norm2_g, w_in,
           q_norm_g, k_norm_g, w_attn_o, conv_w, w_conv_o, ssd_conv_w, ssd_conv_b, ssd_dt_bias, ssd_a_log, ssd_d,
           ssd_norm_g, w_ssd_o, w_merge, ffn_w1, ffn_w2):
    nb_ctx, seq_ctx, d = x_prompt.shape
    nb_lat, seq_lat, _ = x_sample.shape
    past = cache_k.shape[2]
    cond = jnp.concatenate([c_ctx[None, :], c, jnp.zeros((MOD_ROWS - nb_lat - 1, d), F32)], axis=0)
    mod4 = _modulation(cond, ada_w, ada_b).reshape(DEPTH, MOD_ROWS, 1, 6 * d)
    cache_k4 = cache_k.reshape(nb_lat, DEPTH, past, N_KV_HEADS * HEAD_DIM)
    cache_v4 = cache_v.reshape(nb_lat, DEPTH, past, N_KV_HEADS * HEAD_DIM)
    rope_tabs = _rope_tables(seq_lat)
    zero_state = jnp.zeros((nb_ctx, 2, D_STATE, D_INNER), F32)

    y_ctx = x_prompt.reshape(nb_ctx * seq_ctx, d)
    y_lat = x_sample.reshape(nb_lat * seq_lat, d)
    ks, vs, ss = [], [], []
    for l in range(DEPTH):
        wts = _layer_weights(l, norm1_g, norm2_g, w_in, q_norm_g, k_norm_g, w_attn_o, conv_w, w_conv_o,
                             ssd_conv_w, ssd_conv_b, ssd_dt_bias, ssd_a_log, ssd_d, ssd_norm_g, w_ssd_o,
                             w_merge, ffn_w1, ffn_w2)
        y_ctx, (k32, v32), s_fin = _layer(y_ctx, seq_ctx, False, l, mod4, wts, None, None, zero_state, None)
        y_lat, _, _ = _layer(y_lat, seq_lat, True, l, mod4, wts, cache_k4, cache_v4,
                             _states_to_lanes(state_ssd[:, l]), rope_tabs)
        ks.append(k32.reshape(nb_ctx, seq_ctx, N_KV_HEADS, HEAD_DIM))
        vs.append(v32.reshape(nb_ctx, seq_ctx, N_KV_HEADS, HEAD_DIM))
        ss.append(_states_from_lanes(s_fin))
    return (y_ctx.reshape(nb_ctx, seq_ctx, d), y_lat.reshape(nb_lat, seq_lat, d),
            jnp.stack(ks, axis=1), jnp.stack(vs, axis=1), jnp.stack(ss, axis=1))
```

```python
import functools

import jax
import jax.numpy as jnp
from jax import lax
from jax.experimental import pallas as pl
from jax.experimental.pallas import tpu as pltpu

F32 = jnp.float32
BF16 = jnp.bfloat16

D_MODEL = 1024
DEPTH = 2
GRID_W = 64
N_HEADS = 8
N_KV_HEADS = 2
HEAD_DIM = 128
KV_REP = N_HEADS // N_KV_HEADS
ROPE_THETA = 10000.0
CONV_K = 3
D_INNER = 2 * D_MODEL
SSD_HEADDIM = 64
N_SSD_HEADS = D_INNER // SSD_HEADDIM
N_SSD_GROUPS = 8
SSD_HPG = N_SSD_HEADS // N_SSD_GROUPS
D_STATE = 128
CHUNK = 128
D_FF = -(-8 * D_MODEL // 768) * 256
EPS = 1e-6

LANES = 128
BF16_ROWS = 16
VMEM_LIMIT = 56 * 1024 * 1024

XBC_W = 2 * D_INNER
C_XBC = 0
C_Z = C_XBC + XBC_W
C_GATE = C_Z + D_INNER
C_BG = C_GATE + 3 * D_MODEL
C_CG = C_BG + D_MODEL
C_CX = C_CG + D_MODEL
C_Q = C_CX + D_MODEL
C_K = C_Q + D_MODEL
C_V = C_K + N_KV_HEADS * HEAD_DIM
N_MAIN = C_V + N_KV_HEADS * HEAD_DIM
PROJ_TN = N_MAIN - C_Q
J_QKV = C_Q // PROJ_TN
LOG2E = 1.4426950408889634
MOD_ROWS = 16
CTX_ROW = 0
LAT_ROW0 = 1


def _cparams(sem):
    return pltpu.CompilerParams(dimension_semantics=sem, vmem_limit_bytes=VMEM_LIMIT)


def _dot(a, b):
    return jnp.dot(a, b, preferred_element_type=F32)


def _sigmoid(x):
    return 1.0 / (1.0 + jnp.exp(-x))


def _silu(x):
    return x * _sigmoid(x)


def _softplus(x):
    return jnp.maximum(x, 0.0) + jnp.log1p(jnp.exp(-jnp.abs(x)))


def _split3(x):
    hi = x.astype(BF16)
    r1 = x - hi.astype(F32)
    mid = r1.astype(BF16)
    lo = (r1 - mid.astype(F32)).astype(BF16)
    return hi, mid, lo


def _mod_kernel(c_ref, w_ref, b_ref, o_ref):
    s = _silu(c_ref[...]).astype(BF16)
    o_ref[...] = _dot(s, w_ref[...].astype(BF16)) + b_ref[...]


def _modulation(cond, ada_w, ada_b):
    tn = 1536
    return pl.pallas_call(
        _mod_kernel,
        out_shape=jax.ShapeDtypeStruct((DEPTH, MOD_ROWS, 6 * D_MODEL), F32),
        grid=(DEPTH, 6 * D_MODEL // tn),
        in_specs=[
            pl.BlockSpec((MOD_ROWS, D_MODEL), lambda l, j: (0, 0)),
            pl.BlockSpec((None, D_MODEL, tn), lambda l, j: (l, 0, j)),
            pl.BlockSpec((None, 1, tn), lambda l, j: (l, 0, j)),
        ],
        out_specs=pl.BlockSpec((None, MOD_ROWS, tn), lambda l, j: (l, 0, j)),
        compiler_params=_cparams(("parallel", "parallel")),
        name="modulation",
    )(cond, ada_w, ada_b.reshape(DEPTH, 1, 6 * D_MODEL))


def _mod_spec(layer, which, row_fn):
    return pl.BlockSpec((None, None, 1, D_MODEL), lambda *ids: (layer, row_fn(*ids), 0, which))


def _norm_mod(x, g, scale, shift):
    ms = jnp.mean(x * x, axis=-1, keepdims=True)
    return (x * lax.rsqrt(ms + EPS) * g) * (1.0 + scale) + shift


def _head_norm(a, g):
    ms = jnp.mean(a * a, axis=-1, keepdims=True)
    return a * lax.rsqrt(ms + EPS) * g


def _rope(a, cos, sin_signed):
    lane = lax.broadcasted_iota(jnp.int32, a.shape, 1)
    nxt = pltpu.roll(a, HEAD_DIM - 1, 1)
    prv = pltpu.roll(a, 1, 1)
    swapped = jnp.where(lane % 2 == 0, nxt, prv)
    return a * cos + swapped * sin_signed


def _inproj_kernel(latent, x_ref, shift_ref, scale_ref, g_ref, w_ref, wdt_ref, qg_ref, kg_ref, *rest):
    if latent:
        cos_ref, sin_ref, proj_ref, dt_ref, h_scr = rest
    else:
        proj_ref, dt_ref, k32_ref, v32_ref, h_scr = rest
    j = pl.program_id(1)

    @pl.when(j == 0)
    def _():
        h = _norm_mod(x_ref[...], g_ref[...], scale_ref[...], shift_ref[...]).astype(BF16)
        h_scr[...] = h
        dt_ref[...] = _dot(h, wdt_ref[...])

    acc = _dot(h_scr[...], w_ref[...])

    @pl.when(j != J_QKV)
    def _():
        proj_ref[...] = acc.astype(BF16)

    @pl.when(j == J_QKV)
    def _():
        q_scale = HEAD_DIM ** -0.5 * LOG2E
        for hh in range(N_HEADS):
            sl = slice(hh * HEAD_DIM, (hh + 1) * HEAD_DIM)
            q = _head_norm(acc[:, sl], qg_ref[...])
            if latent:
                q = _rope(q, cos_ref[...], sin_ref[...])
            proj_ref[:, sl] = (q * q_scale).astype(BF16)
        k0 = C_K - C_Q
        for hh in range(N_KV_HEADS):
            sl = slice(k0 + hh * HEAD_DIM, k0 + (hh + 1) * HEAD_DIM)
            k = _head_norm(acc[:, sl], kg_ref[...])
            if latent:
                k = _rope(k, cos_ref[...], sin_ref[...])
            else:
                k32_ref[:, hh * HEAD_DIM:(hh + 1) * HEAD_DIM] = k
            proj_ref[:, sl] = k.astype(BF16)
        v0 = C_V - C_Q
        v = acc[:, v0:]
        if not latent:
            v32_ref[...] = v
        proj_ref[:, v0:] = v.astype(BF16)


def _in_proj(x, mod4, layer, seq, latent, norm_g, w_main, w_dt, q_g, k_g, rope_tabs):
    t = x.shape[0]
    tm = 1024
    tiles_per_seq = max(seq // tm, 1)
    row_fn = (lambda i, j: LAT_ROW0 + i // tiles_per_seq) if latent else (lambda i, j: CTX_ROW)
    in_specs = [
        pl.BlockSpec((tm, D_MODEL), lambda i, j: (i, 0)),
        _mod_spec(layer, 0, row_fn),
        _mod_spec(layer, 1, row_fn),
        pl.BlockSpec((1, D_MODEL), lambda i, j: (0, 0)),
        pl.BlockSpec((D_MODEL, PROJ_TN), lambda i, j: (0, j)),
        pl.BlockSpec((D_MODEL, LANES), lambda i, j: (0, 0)),
        pl.BlockSpec((1, HEAD_DIM), lambda i, j: (0, 0)),
        pl.BlockSpec((1, HEAD_DIM), lambda i, j: (0, 0)),
    ]
    args = [x, mod4, mod4, norm_g, w_main, w_dt, q_g, k_g]
    out_shape = [jax.ShapeDtypeStruct((t, N_MAIN), BF16), jax.ShapeDtypeStruct((t, LANES), F32)]
    out_specs = [pl.BlockSpec((tm, PROJ_TN), lambda i, j: (i, j)), pl.BlockSpec((tm, LANES), lambda i, j: (i, 0))]
    if latent:
        tab_spec = pl.BlockSpec((tm, HEAD_DIM), lambda i, j: (i % tiles_per_seq, 0))
        in_specs += [tab_spec, tab_spec]
        args += list(rope_tabs)
    else:
        kv_w = N_KV_HEADS * HEAD_DIM
        out_shape += [jax.ShapeDtypeStruct((t, kv_w), F32)] * 2
        out_specs += [pl.BlockSpec((tm, kv_w), lambda i, j: (i, 0))] * 2
    return pl.pallas_call(
        functools.partial(_inproj_kernel, latent),
        out_shape=out_shape,
        grid=(t // tm, N_MAIN // PROJ_TN),
        in_specs=in_specs,
        out_specs=out_specs,
        scratch_shapes=[pltpu.VMEM((tm, D_MODEL), BF16)],
        compiler_params=_cparams(("parallel", "arbitrary")),
        name="in_proj_lat" if latent else "in_proj_ctx",
    )(*args)


def _attn_kernel(latent, q_ref, k_ref, v_ref, *rest):
    if latent:
        ck_ref, cv_ref, o_ref = rest
    else:
        (o_ref,) = rest
    nt = (((1,), (1,)), ((), ()))

    def with_ones(v):
        return jnp.concatenate([v, jnp.ones_like(v)], axis=1)

    k = k_ref[...]
    v1 = with_ones(v_ref[...])
    if latent:
        ck = ck_ref[...].astype(BF16)
        cv1 = with_ones(cv_ref[...].astype(BF16))
    for hh in range(KV_REP):
        sl = slice(hh * HEAD_DIM, (hh + 1) * HEAD_DIM)
        q = q_ref[:, sl]
        s = lax.dot_general(q, k, nt, preferred_element_type=F32)
        m = jnp.max(s, axis=-1, keepdims=True)
        if latent:
            s2 = lax.dot_general(q, ck, nt, preferred_element_type=F32)
            m = jnp.maximum(m, jnp.max(s2, axis=-1, keepdims=True))
        o = _dot(jnp.exp2(s - m).astype(BF16), v1)
        if latent:
            o = o + _dot(jnp.exp2(s2 - m).astype(BF16), cv1)
        o_ref[:, sl] = (o[:, :HEAD_DIM] / o[:, HEAD_DIM:]).astype(BF16)


def _attention(proj, seq, latent, layer, cache_k4, cache_v4):
    t = proj.shape[0]
    nb = t // seq
    tq = min(512, seq)
    qt = seq // tq
    gw = KV_REP * HEAD_DIM
    in_specs = [
        pl.BlockSpec((tq, gw), lambda b, g, i: (b * qt + i, C_Q // gw + g)),
        pl.BlockSpec((seq, HEAD_DIM), lambda b, g, i: (b, C_K // HEAD_DIM + g)),
        pl.BlockSpec((seq, HEAD_DIM), lambda b, g, i: (b, C_V // HEAD_DIM + g)),
    ]
    args = [proj, proj, proj]
    if latent:
        past = cache_k4.shape[2]
        cspec = pl.BlockSpec((None, None, past, HEAD_DIM), lambda b, g, i: (b, layer, 0, g))
        in_specs += [cspec, cspec]
        args += [cache_k4, cache_v4]
    return pl.pallas_call(
        functools.partial(_attn_kernel, latent),
        out_shape=jax.ShapeDtypeStruct((t, N_HEADS * HEAD_DIM), BF16),
        grid=(nb, N_KV_HEADS, qt),
        in_specs=in_specs,
        out_specs=pl.BlockSpec((tq, gw), lambda b, g, i: (b * qt + i, g)),
        compiler_params=_cparams(("parallel", "parallel", "arbitrary")),
        name="attention_lat" if latent else "attention_ctx",
    )(*args)


def _conv3(p, prev_row, next_row, w, first, last):
    tm = p.shape[0]
    sub = 8
    rid = lax.broadcasted_iota(jnp.int32, (sub, p.shape[1]), 0)
    prev_row = jnp.where(first, 0.0, prev_row)
    next_row = jnp.where(last, 0.0, next_row)
    dn = pltpu.roll(p, 1, 0)
    up = pltpu.roll(p, tm - 1, 0)
    dn = jnp.concatenate([jnp.where(rid == 0, prev_row, dn[:sub, :]), dn[sub:, :]], axis=0)
    up = jnp.concatenate([up[:tm - sub, :], jnp.where(rid == sub - 1, next_row, up[tm - sub:, :])], axis=0)
    return w[0:1, :] * dn + w[1:2, :] * p + w[2:3, :] * up


GW = SSD_HPG * SSD_HEADDIM
NBC = N_SSD_GROUPS * D_STATE
CONV_SLAB = 1024


def _ssd_chunk(rev, xcols, dtr_ref, dtrt_ref, brow_ref, bcol_ref, arow_ref, acol_ref, st_scr, d_ref, emit):
    ii = lax.broadcasted_iota(jnp.int32, (CHUNK, CHUNK), 0)
    jj = lax.broadcasted_iota(jnp.int32, (CHUNK, CHUNK), 1)
    keep = (ii <= jj) if rev else (ii >= jj)
    tri = jnp.where(keep, 1.0, 0.0).astype(BF16)
    tri_t = jnp.where((jj <= ii) if rev else (jj >= ii), 1.0, 0.0).astype(BF16)

    dt = _softplus(dtr_ref[...] + brow_ref[...])
    acs = sum(_dot(tri, part) for part in _split3(dt * (-jnp.exp(arow_ref[...]))))
    dt_t = _softplus(dtrt_ref[...] + bcol_ref[...])
    acs_t = sum(_dot(part, tri_t) for part in _split3(dt_t * (-jnp.exp(acol_ref[...]))))

    edge = 0 if rev else CHUNK - 1
    ea = jnp.exp(acs)
    dec_row = ea[edge:edge + 1, :]
    w_t = dt_t * jnp.exp(acs_t[:, edge:edge + 1] - acs_t)
    acs2 = acs * LOG2E
    l2_t = acs_t * LOG2E - jnp.log2(dt_t)
    head_of_lane = lax.broadcasted_iota(jnp.int32, (CHUNK, GW), 1) // SSD_HEADDIM
    head_of_lane_row = lax.broadcasted_iota(jnp.int32, (1, GW), 1) // SSD_HEADDIM
    lane0 = N_SSD_HEADS if rev else 0
    zero_blk = jnp.zeros((D_STATE, CHUNK), BF16)

    for g in range(N_SSD_GROUPS):
        gs = slice(g * GW, (g + 1) * GW)
        xs = xcols(g * GW, (g + 1) * GW)
        bc = xcols(D_INNER + g * D_STATE, D_INNER + (g + 1) * D_STATE)
        cc = xcols(D_INNER + NBC + g * D_STATE, D_INNER + NBC + (g + 1) * D_STATE)
        bc_t = bc.astype(F32).T
        cc_f = cc.astype(F32)
        cb = _dot(cc, bc_t.astype(BF16))
        st = st_scr[:, gs]
        st_b = st.astype(BF16)
        acc = None
        dec_w = jnp.zeros((1, GW), F32)
        for r in range(SSD_HPG):
            ln = lane0 + g * SSD_HPG + r
            seg = acs2[:, ln:ln + 1] - l2_t[ln:ln + 1, :]
            m = cb * jnp.exp2(jnp.where(keep, seg, -jnp.inf))
            cce = cc_f * ea[:, ln:ln + 1]
            bw = bc_t * w_t[ln:ln + 1, :]
            lhs = jnp.concatenate([jnp.concatenate([m.astype(BF16), cce.astype(BF16)], axis=1),
                                   jnp.concatenate([bw.astype(BF16), zero_blk], axis=1)], axis=0)
            sel = head_of_lane == r
            rhs = jnp.concatenate([jnp.where(sel, xs, jnp.zeros_like(xs)),
                                   jnp.where(sel, st_b, jnp.zeros_like(st_b))], axis=0)
            part = _dot(lhs, rhs)
            acc = part if acc is None else acc + part
            dec_w = jnp.where(head_of_lane_row == r, dec_row[:, ln:ln + 1], dec_w)
        y = acc[:CHUNK, :]
        if not rev:
            y = y + d_ref[:, gs] * xs.astype(F32)
        emit(g, y)
        st_scr[:, gs] = st * dec_w + acc[CHUNK:, :]


def _ssd_kernel(nchunks, has_init, emit_final, xbc_ref, xp_ref, xn_ref, cw_ref, cb_ref, dtr_ref, dtrt_ref,
                brow_ref, bcol_ref, arow_ref, acol_ref, d_ref, *rest):
    rest = list(rest)
    s0_ref = rest.pop(0) if has_init else None
    y_ref = rest.pop(0)
    sfin_ref = rest.pop(0) if emit_final else None
    xact_scr, yf_scr, st_scr = rest
    s = pl.program_id(1)
    scan_args = (dtr_ref, dtrt_ref, brow_ref, bcol_ref, arow_ref, acol_ref, st_scr, d_ref)

    def load_state(d):
        if not has_init:
            st_scr[...] = jnp.zeros_like(st_scr)
            return
        for k in range(D_INNER // LANES):
            sl = slice(k * LANES, (k + 1) * LANES)
            st_scr[:, sl] = s0_ref[d, sl, :].T

    def store_state(d):
        if not emit_final:
            return
        for k in range(D_INNER // LANES):
            sl = slice(k * LANES, (k + 1) * LANES)
            sfin_ref[d, sl, :] = st_scr[:, sl].T

    @pl.when(s == 0)
    def _():
        load_state(0)

    @pl.when(s < nchunks)
    def _():
        c = s
        r = BF16_ROWS - 1
        for k in range(XBC_W // CONV_SLAB):
            sl = slice(k * CONV_SLAB, (k + 1) * CONV_SLAB)
            conv = _conv3(xbc_ref[:, sl].astype(F32), xp_ref[r:r + 1, sl].astype(F32),
                          xn_ref[0:1, sl].astype(F32), cw_ref[:, sl], c == 0, c == nchunks - 1)
            xact_scr[c, :, sl] = _silu(conv + cb_ref[:, sl]).astype(BF16)

        def emit(g, y):
            yf_scr[c, :, g * GW:(g + 1) * GW] = y.astype(BF16)

        _ssd_chunk(False, lambda lo, hi: xact_scr[c, :, lo:hi], *scan_args, emit)

    @pl.when(s == nchunks - 1)
    def _():
        store_state(0)

    @pl.when(s == nchunks)
    def _():
        load_state(1)

    @pl.when(s >= nchunks)
    def _():
        c = 2 * nchunks - 1 - s

        def emit(g, y):
            gs = slice(g * GW, (g + 1) * GW)
            y_ref[:, gs] = (y + yf_scr[c, :, gs].astype(F32)).astype(BF16)

        _ssd_chunk(True, lambda lo, hi: xact_scr[c, :, lo:hi], *scan_args, emit)

    @pl.when(s == 2 * nchunks - 1)
    def _():
        store_state(1)


def _ssd(proj, dt_raw, dt_raw_t, seq, conv_w, conv_b, bias, a_log, d_wide, layer, s0, emit_final):
    t = proj.shape[0]
    nb = t // seq
    nchunks = seq // CHUNK
    rpc = CHUNK // BF16_ROWS
    last_blk = t // BF16_ROWS - 1
    xb = C_XBC // XBC_W

    def chunk_in(b, s):
        return b * nchunks + jnp.minimum(s, nchunks - 1)

    def chunk_of(b, s):
        return b * nchunks + jnp.where(s < nchunks, s, 2 * nchunks - 1 - s)

    def chunk_out(b, s):
        return b * nchunks + jnp.where(s < nchunks, nchunks - 1, 2 * nchunks - 1 - s)

    pad = jnp.zeros((LANES - 2 * N_SSD_HEADS,), F32)
    b_row = jnp.concatenate([bias.reshape(-1), pad]).reshape(1, LANES)
    a_row = jnp.concatenate([a_log.reshape(-1), pad]).reshape(1, LANES)
    const = lambda b, s: (0, 0)
    in_specs = [
        pl.BlockSpec((CHUNK, XBC_W), lambda b, s: (chunk_in(b, s), xb)),
        pl.BlockSpec((BF16_ROWS, XBC_W), lambda b, s: (jnp.maximum(chunk_in(b, s) * rpc - 1, 0), xb)),
        pl.BlockSpec((BF16_ROWS, XBC_W), lambda b, s: (jnp.minimum((chunk_in(b, s) + 1) * rpc, last_blk), xb)),
        pl.BlockSpec((CONV_K, XBC_W), const),
        pl.BlockSpec((1, XBC_W), const),
        pl.BlockSpec((CHUNK, LANES), lambda b, s: (chunk_of(b, s), 0)),
        pl.BlockSpec((LANES, CHUNK), lambda b, s: (0, chunk_of(b, s))),
        pl.BlockSpec((1, LANES), const), pl.BlockSpec((LANES, 1), const),
        pl.BlockSpec((1, LANES), const), pl.BlockSpec((LANES, 1), const),
        pl.BlockSpec((1, D_INNER), const),
    ]
    args = [proj, proj, proj, conv_w, conv_b, dt_raw, dt_raw_t, b_row, b_row.reshape(LANES, 1), a_row,
            a_row.reshape(LANES, 1), d_wide]
    out_shape = [jax.ShapeDtypeStruct((t, D_INNER), BF16)]
    out_specs = [pl.BlockSpec((CHUNK, D_INNER), lambda b, s: (chunk_out(b, s), 0))]
    if s0 is not None:
        in_specs.append(pl.BlockSpec((None, None, 2, D_INNER, D_STATE), lambda b, s: (b, layer, 0, 0, 0)))
        args.append(s0)
    if emit_final:
        out_shape.append(jax.ShapeDtypeStruct((nb, 2, D_INNER, D_STATE), F32))
        out_specs.append(pl.BlockSpec((None, 2, D_INNER, D_STATE), lambda b, s: (b, 0, 0, 0)))
    return pl.pallas_call(
        functools.partial(_ssd_kernel, nchunks, s0 is not None, emit_final),
        out_shape=out_shape,
        grid=(nb, 2 * nchunks),
        in_specs=in_specs,
        out_specs=out_specs,
        scratch_shapes=[pltpu.VMEM((nchunks, CHUNK, XBC_W), BF16), pltpu.VMEM((nchunks, CHUNK, D_INNER), BF16),
                        pltpu.VMEM((D_STATE, D_INNER), F32)],
        compiler_params=_cparams(("parallel", "arbitrary")),
        name="ssd_lat" if s0 is not None else "ssd_ctx",
    )(*args)


def _merge_kernel(tm, seq, x_ref, gate_ref, attn_ref, bg_ref, cg_ref, cx_ref, cgp_ref, cxp_ref, cgn_ref, cxn_ref,
                  cw_ref, y_ref, z_ref, gl_ref, ng_ref, wa_ref, wc_ref, ws_ref, wm_ref, o_ref):
    i = pl.program_id(0)
    tps = seq // tm
    r = BF16_ROWS - 1
    p = cg_ref[...].astype(F32) * cx_ref[...].astype(F32)
    prev_row = cgp_ref[r:r + 1, :].astype(F32) * cxp_ref[r:r + 1, :].astype(F32)
    next_row = cgn_ref[0:1, :].astype(F32) * cxn_ref[0:1, :].astype(F32)
    conv = _conv3(p, prev_row, next_row, cw_ref[...], (i % tps) == 0, (i % tps) == tps - 1)
    u = (bg_ref[...].astype(F32) * conv).astype(BF16)
    tg = y_ref[...].astype(F32) * _silu(z_ref[...].astype(F32))
    ms = jnp.mean(tg * tg, axis=-1, keepdims=True)
    yn = (tg * lax.rsqrt(ms + EPS) * ng_ref[...]).astype(BF16)
    y_ssd = _dot(yn, ws_ref[...])
    y_attn = _dot(attn_ref[...], wa_ref[...])
    y_conv = _dot(u, wc_ref[...])
    d = D_MODEL
    mix = (_sigmoid(gl_ref[:, 0:d].astype(F32)) * y_attn
           + _sigmoid(gl_ref[:, d:2 * d].astype(F32)) * y_conv
           + _sigmoid(gl_ref[:, 2 * d:3 * d].astype(F32)) * y_ssd)
    o_ref[...] = x_ref[...] + gate_ref[...] * _dot(mix.astype(BF16), wm_ref[...])


def _resident(shape):
    return pl.BlockSpec(shape, lambda *_: (0,) * len(shape), pipeline_mode=pl.Buffered(1))


def _merge(x, mod4, layer, seq, latent, proj, attn, y, conv_w, norm_g, wa, wc, ws, wm):
    t = x.shape[0]
    tm = 256
    tps = seq // tm
    row_fn = (lambda i: LAT_ROW0 + i // tps) if latent else (lambda i: CTX_ROW)
    d = D_MODEL
    rpt = tm // BF16_ROWS
    last_blk = t // BF16_ROWS - 1

    def tile(width, col_block):
        return pl.BlockSpec((tm, width), lambda i: (i, col_block))

    def prev(cb):
        return pl.BlockSpec((BF16_ROWS, d), lambda i: (jnp.maximum(i * rpt - 1, 0), cb))

    def nxt(cb):
        return pl.BlockSpec((BF16_ROWS, d), lambda i: (jnp.minimum((i + 1) * rpt, last_blk), cb))

    return pl.pallas_call(
        functools.partial(_merge_kernel, tm, seq),
        out_shape=jax.ShapeDtypeStruct((t, d), F32),
        grid=(t // tm,),
        in_specs=[
            tile(d, 0),
            _mod_spec(layer, 2, row_fn),
            tile(d, 0),
            tile(d, C_BG // d), tile(d, C_CG // d), tile(d, C_CX // d),
            prev(C_CG // d), prev(C_CX // d), nxt(C_CG // d), nxt(C_CX // d),
            _resident((CONV_K, d)),
            tile(D_INNER, 0),
            tile(D_INNER, C_Z // D_INNER),
            tile(3 * d, C_GATE // (3 * d)),
            _resident((1, D_INNER)),
            _resident((d, d)), _resident((d, d)), _resident((D_INNER, d)), _resident((d, d)),
        ],
        out_specs=tile(d, 0),
        compiler_params=_cparams(("parallel",)),
        name="merge",
    )(x, mod4, attn, proj, proj, proj, proj, proj, proj, proj, conv_w, y, proj, proj, norm_g, wa, wc, ws, wm)


FF_SPLIT = 2


def _ffn_kernel(x_ref, shift_ref, scale_ref, gate_ref, g_ref, w1_ref, w2_ref, o_ref):
    x = x_ref[...]
    h = _norm_mod(x, g_ref[...], scale_ref[...], shift_ref[...]).astype(BF16)
    slab = D_FF // FF_SPLIT
    acc = None
    for j in range(FF_SPLIT):
        hg = _dot(h, w1_ref[:, j * slab:(j + 1) * slab])
        hu = _dot(h, w1_ref[:, D_FF + j * slab:D_FF + (j + 1) * slab])
        part = _dot((_silu(hg) * hu).astype(BF16), w2_ref[j * slab:(j + 1) * slab, :])
        acc = part if acc is None else acc + part
    o_ref[...] = x + gate_ref[...] * acc


def _ffn(x, mod4, layer, seq, latent, norm_g, w1, w2):
    t = x.shape[0]
    tm = 512
    tps = max(seq // tm, 1)
    row_fn = (lambda i: LAT_ROW0 + i // tps) if latent else (lambda i: CTX_ROW)
    return pl.pallas_call(
        _ffn_kernel,
        out_shape=jax.ShapeDtypeStruct((t, D_MODEL), F32),
        grid=(t // tm,),
        in_specs=[
            pl.BlockSpec((tm, D_MODEL), lambda i: (i, 0)),
            _mod_spec(layer, 3, row_fn),
            _mod_spec(layer, 4, row_fn),
            _mod_spec(layer, 5, row_fn),
            _resident((1, D_MODEL)),
            _resident((D_MODEL, 2 * D_FF)),
            _resident((D_FF, D_MODEL)),
        ],
        out_specs=pl.BlockSpec((tm, D_MODEL), lambda i: (i, 0)),
        compiler_params=_cparams(("parallel",)),
        name="ffn",
    )(x, mod4, mod4, mod4, norm_g, w1, w2)


def _rope_tables(n_tok):
    rows = n_tok // GRID_W
    row = jnp.repeat(jnp.arange(rows, dtype=F32), GRID_W)
    col = jnp.tile(jnp.arange(GRID_W, dtype=F32), rows)
    axis_dim = HEAD_DIM // 2
    inv_freq = 1.0 / (ROPE_THETA ** (jnp.arange(0, axis_dim, 2, dtype=F32) / axis_dim))
    ang = jnp.concatenate([row[:, None] * inv_freq, col[:, None] * inv_freq], axis=-1)
    cos = jnp.repeat(jnp.cos(ang), 2, axis=-1)
    sin = jnp.repeat(jnp.sin(ang), 2, axis=-1)
    sign = jnp.tile(jnp.asarray([-1.0, 1.0], F32), HEAD_DIM // 2)
    return cos, sin * sign


def _layer(x, seq, latent, layer, mod4, wts, cache_k4, cache_v4, s0, rope_tabs):
    proj, dt_raw, *kv = _in_proj(x, mod4, layer, seq, latent, wts["norm1_g"], wts["w_main"], wts["w_dt"],
                                 wts["q_g"], wts["k_g"], rope_tabs)
    attn = _attention(proj, seq, latent, layer, cache_k4, cache_v4)
    y, *s_fin = _ssd(proj, dt_raw, dt_raw.T, seq, wts["ssd_conv_w"], wts["ssd_conv_b"], wts["dt_bias"],
                     wts["a_log"], wts["d_wide"], layer, s0, not latent)
    x = _merge(x, mod4, layer, seq, latent, proj, attn, y, wts["conv_w"], wts["ssd_norm_g"],
               wts["w_attn_o"], wts["w_conv_o"], wts["w_ssd_o"], wts["w_merge"])
    x = _ffn(x, mod4, layer, seq, latent, wts["norm2_g"], wts["ffn_w1"], wts["ffn_w2"])
    return x, kv, s_fin


def _layer_weights(l, norm1_g, norm2_g, w_in, q_norm_g, k_norm_g, w_attn_o, conv_w, w_conv_o, ssd_conv_w,
                   ssd_conv_b, ssd_dt_bias, ssd_a_log, ssd_d, ssd_norm_g, w_ssd_o, w_merge, ffn_w1, ffn_w2):
    d = D_MODEL
    kvw = N_KV_HEADS * HEAD_DIM
    o_q, o_k, o_v, o_bg, o_cg, o_cx = 0, d, d + kvw, d + 2 * kvw, 2 * d + 2 * kvw, 3 * d + 2 * kvw
    o_z = o_cx + d
    o_xbc = o_z + D_INNER
    o_dt = o_xbc + XBC_W
    o_gate = o_dt + 2 * N_SSD_HEADS
    w = w_in[l]
    cols = lambda a, n: w[:, a:a + n]
    w_main = jnp.concatenate([cols(o_xbc, XBC_W), cols(o_z, D_INNER), cols(o_gate, 3 * d), cols(o_bg, d),
                              cols(o_cg, d), cols(o_cx, d), cols(o_q, d), cols(o_k, kvw), cols(o_v, kvw)],
                             axis=1).astype(BF16)
    w_dt = jnp.concatenate([cols(o_dt, 2 * N_SSD_HEADS), jnp.zeros((d, LANES - 2 * N_SSD_HEADS), F32)],
                           axis=1).astype(BF16)
    return {
        "norm1_g": norm1_g[l].reshape(1, d), "norm2_g": norm2_g[l].reshape(1, d),
        "w_main": w_main, "w_dt": w_dt,
        "q_g": q_norm_g[l].reshape(1, HEAD_DIM), "k_g": k_norm_g[l].reshape(1, HEAD_DIM),
        "w_attn_o": w_attn_o[l].astype(BF16), "conv_w": conv_w[l], "w_conv_o": w_conv_o[l].astype(BF16),
        "ssd_conv_w": ssd_conv_w[l], "ssd_conv_b": ssd_conv_b[l].reshape(1, -1),
        "dt_bias": ssd_dt_bias[l], "a_log": ssd_a_log[l],
        "d_wide": jnp.repeat(ssd_d[l], SSD_HEADDIM).reshape(1, D_INNER),
        "ssd_norm_g": ssd_norm_g[l].reshape(1, D_INNER), "w_ssd_o": w_ssd_o[l].astype(BF16),
        "w_merge": w_merge[l].astype(BF16), "ffn_w1": ffn_w1[l].astype(BF16), "ffn_w2": ffn_w2[l].astype(BF16),
    }


def kernel(x_prompt, x_sample, c, cache_k, cache_v, state_ssd, c_ctx, ada_w, ada_b, norm1_g, norm2_g, w_in,
           q_norm_g, k_norm_g, w_attn_o, conv_w, w_conv_o, ssd_conv_w, ssd_conv_b, ssd_dt_bias, ssd_a_log, ssd_d,
           ssd_norm_g, w_ssd_o, w_merge, ffn_w1, ffn_w2):
    nb_ctx, seq_ctx, d = x_prompt.shape
    nb_lat, seq_lat, _ = x_sample.shape
    past = cache_k.shape[2]
    cond = jnp.concatenate([c_ctx[None, :], c, jnp.zeros((MOD_ROWS - nb_lat - 1, d), F32)], axis=0)
    mod4 = _modulation(cond, ada_w, ada_b).reshape(DEPTH, MOD_ROWS, 1, 6 * d)
    cache_k4 = cache_k.reshape(nb_lat, DEPTH, past, N_KV_HEADS * HEAD_DIM)
    cache_v4 = cache_v.reshape(nb_lat, DEPTH, past, N_KV_HEADS * HEAD_DIM)
    rope_tabs = _rope_tables(seq_lat)
    s0_lat = state_ssd.reshape(nb_lat, DEPTH, 2, D_INNER, D_STATE)

    y_ctx = x_prompt.reshape(nb_ctx * seq_ctx, d)
    y_lat = x_sample.reshape(nb_lat * seq_lat, d)
    ks, vs, ss = [], [], []
    for l in range(DEPTH):
        wts = _layer_weights(l, norm1_g, norm2_g, w_in, q_norm_g, k_norm_g, w_attn_o, conv_w, w_conv_o,
                             ssd_conv_w, ssd_conv_b, ssd_dt_bias, ssd_a_log, ssd_d, ssd_norm_g, w_ssd_o,
                             w_merge, ffn_w1, ffn_w2)
        y_ctx, (k32, v32), (s_fin,) = _layer(y_ctx, seq_ctx, False, l, mod4, wts, None, None, None, None)
        y_lat, _, _ = _layer(y_lat, seq_lat, True, l, mod4, wts, cache_k4, cache_v4, s0_lat, rope_tabs)
        ks.append(k32.reshape(nb_ctx, seq_ctx, N_KV_HEADS, HEAD_DIM))
        vs.append(v32.reshape(nb_ctx, seq_ctx, N_KV_HEADS, HEAD_DIM))
        ss.append(s_fin.reshape(nb_ctx, 2, N_SSD_HEADS, SSD_HEADDIM, D_STATE))
    return (y_ctx.reshape(nb_ctx, seq_ctx, d), y_lat.reshape(nb_lat, seq_lat, d),
            jnp.stack(ks, axis=1), jnp.stack(vs, axis=1), jnp.stack(ss, axis=1))
```

```python
import functools

import jax
import jax.numpy as jnp
from jax import lax
from jax.experimental import pallas as pl
from jax.experimental.pallas import tpu as pltpu

F32 = jnp.float32
BF16 = jnp.bfloat16

D_MODEL = 1024
DEPTH = 2
GRID_W = 64
N_HEADS = 8
N_KV_HEADS = 2
HEAD_DIM = 128
KV_REP = N_HEADS // N_KV_HEADS
ROPE_THETA = 10000.0
CONV_K = 3
D_INNER = 2 * D_MODEL
SSD_HEADDIM = 64
N_SSD_HEADS = D_INNER // SSD_HEADDIM
N_SSD_GROUPS = 8
SSD_HPG = N_SSD_HEADS // N_SSD_GROUPS
D_STATE = 128
CHUNK = 128
D_FF = -(-8 * D_MODEL // 768) * 256
EPS = 1e-6

LANES = 128
BF16_ROWS = 16
VMEM_LIMIT = 56 * 1024 * 1024

XBC_W = 2 * D_INNER
C_XBC = 0
C_Z = C_XBC + XBC_W
C_GATE = C_Z + D_INNER
C_BG = C_GATE + 3 * D_MODEL
C_CG = C_BG + D_MODEL
C_CX = C_CG + D_MODEL
C_Q = C_CX + D_MODEL
C_K = C_Q + D_MODEL
C_V = C_K + N_KV_HEADS * HEAD_DIM
N_MAIN = C_V + N_KV_HEADS * HEAD_DIM
PROJ_TN = N_MAIN - C_Q
J_QKV = C_Q // PROJ_TN
LOG2E = 1.4426950408889634
MOD_ROWS = 16
CTX_ROW = 0
LAT_ROW0 = 1


def _cparams(sem):
    return pltpu.CompilerParams(dimension_semantics=sem, vmem_limit_bytes=VMEM_LIMIT)


def _dot(a, b):
    return jnp.dot(a, b, preferred_element_type=F32)


def _sigmoid(x):
    return 1.0 / (1.0 + jnp.exp(-x))


def _silu(x):
    return x * _sigmoid(x)


def _softplus(x):
    return jnp.maximum(x, 0.0) + jnp.log1p(jnp.exp(-jnp.abs(x)))


def _split3(x):
    hi = x.astype(BF16)
    r1 = x - hi.astype(F32)
    mid = r1.astype(BF16)
    lo = (r1 - mid.astype(F32)).astype(BF16)
    return hi, mid, lo


def _mod_kernel(c_ref, w_ref, b_ref, o_ref):
    s = _silu(c_ref[...]).astype(BF16)
    o_ref[...] = _dot(s, w_ref[...].astype(BF16)) + b_ref[...]


def _modulation(cond, ada_w, ada_b):
    tn = 1536
    return pl.pallas_call(
        _mod_kernel,
        out_shape=jax.ShapeDtypeStruct((DEPTH, MOD_ROWS, 6 * D_MODEL), F32),
        grid=(DEPTH, 6 * D_MODEL // tn),
        in_specs=[
            pl.BlockSpec((MOD_ROWS, D_MODEL), lambda l, j: (0, 0)),
            pl.BlockSpec((None, D_MODEL, tn), lambda l, j: (l, 0, j)),
            pl.BlockSpec((None, 1, tn), lambda l, j: (l, 0, j)),
        ],
        out_specs=pl.BlockSpec((None, MOD_ROWS, tn), lambda l, j: (l, 0, j)),
        compiler_params=_cparams(("parallel", "parallel")),
        name="modulation",
    )(cond, ada_w, ada_b.reshape(DEPTH, 1, 6 * D_MODEL))


def _mod_spec(layer, which, row_fn):
    return pl.BlockSpec((None, None, 1, D_MODEL), lambda *ids: (layer, row_fn(*ids), 0, which))


def _norm_mod(x, g, scale, shift):
    ms = jnp.mean(x * x, axis=-1, keepdims=True)
    return (x * lax.rsqrt(ms + EPS) * g) * (1.0 + scale) + shift


def _head_norm(a, g):
    ms = jnp.mean(a * a, axis=-1, keepdims=True)
    return a * lax.rsqrt(ms + EPS) * g


def _rope(a, cos, sin_signed):
    lane = lax.broadcasted_iota(jnp.int32, a.shape, 1)
    nxt = pltpu.roll(a, HEAD_DIM - 1, 1)
    prv = pltpu.roll(a, 1, 1)
    swapped = jnp.where(lane % 2 == 0, nxt, prv)
    return a * cos + swapped * sin_signed


def _inproj_kernel(latent, x_ref, shift_ref, scale_ref, g_ref, w_ref, wdt_ref, proj_ref, dt_ref, qk_ref, vb_ref,
                   *rest):
    if latent:
        (h_scr,) = rest
    else:
        v32_ref, h_scr = rest
    j = pl.program_id(1)

    @pl.when(j == 0)
    def _():
        h = _norm_mod(x_ref[...], g_ref[...], scale_ref[...], shift_ref[...]).astype(BF16)
        h_scr[...] = h
        dt_ref[...] = _dot(h, wdt_ref[...])

    def activation(col):
        if C_Z <= col < C_GATE:
            return _silu
        if C_GATE <= col < C_BG:
            return _sigmoid
        return None

    for jt in range(J_QKV):
        c0, c1 = jt * PROJ_TN, (jt + 1) * PROJ_TN
        cuts = [c0] + [c for c in (C_Z, C_GATE, C_BG) if c0 < c < c1] + [c1]

        @pl.when(j == jt)
        def _(c0=c0, cuts=cuts):
            for a, b in zip(cuts[:-1], cuts[1:]):
                acc = _dot(h_scr[...], w_ref[:, a - c0:b - c0])
                fn = activation(a)
                proj_ref[:, a - c0:b - c0] = (acc if fn is None else fn(acc)).astype(BF16)

    @pl.when(j == J_QKV)
    def _():
        acc = _dot(h_scr[...], w_ref[...])
        v0 = C_V - C_Q
        qk_ref[...] = acc[:, :v0]
        if not latent:
            v32_ref[...] = acc[:, v0:]
        vb_ref[...] = acc[:, v0:].astype(BF16)


QK_W = C_V - C_Q
KV_W = N_KV_HEADS * HEAD_DIM


def _in_proj(x, mod4, layer, seq, latent, norm_g, w_main, w_dt):
    t = x.shape[0]
    tm = 1024
    tiles_per_seq = max(seq // tm, 1)
    row_fn = (lambda i, j: LAT_ROW0 + i // tiles_per_seq) if latent else (lambda i, j: CTX_ROW)
    once = lambda width: pl.BlockSpec((tm, width), lambda i, j: (i, 0))
    in_specs = [
        pl.BlockSpec((tm, D_MODEL), lambda i, j: (i, 0)),
        _mod_spec(layer, 0, row_fn),
        _mod_spec(layer, 1, row_fn),
        pl.BlockSpec((1, D_MODEL), lambda i, j: (0, 0)),
        pl.BlockSpec((D_MODEL, PROJ_TN), lambda i, j: (0, j)),
        pl.BlockSpec((D_MODEL, LANES), lambda i, j: (0, 0)),
    ]
    out_shape = [jax.ShapeDtypeStruct((t, C_Q), BF16), jax.ShapeDtypeStruct((t, LANES), F32),
                 jax.ShapeDtypeStruct((t, QK_W), F32), jax.ShapeDtypeStruct((t, KV_W), BF16)]
    out_specs = [pl.BlockSpec((tm, PROJ_TN), lambda i, j: (i, jnp.minimum(j, J_QKV - 1))),
                 once(LANES), once(QK_W), once(KV_W)]
    if not latent:
        out_shape.append(jax.ShapeDtypeStruct((t, KV_W), F32))
        out_specs.append(once(KV_W))
    return pl.pallas_call(
        functools.partial(_inproj_kernel, latent),
        out_shape=out_shape,
        grid=(t // tm, N_MAIN // PROJ_TN),
        in_specs=in_specs,
        out_specs=out_specs,
        scratch_shapes=[pltpu.VMEM((tm, D_MODEL), BF16)],
        compiler_params=_cparams(("parallel", "arbitrary")),
        name="in_proj_lat" if latent else "in_proj_ctx",
    )(x, mod4, mod4, norm_g, w_main, w_dt)


def _attn_kernel(latent, q_ref, k_ref, v_ref, qg_ref, kg_ref, *rest):
    if latent:
        cosq_ref, sinq_ref, cosk_ref, sink_ref, ck_ref, cv_ref, o_ref, k_scr, v1_scr = rest
    else:
        o_ref, knew_ref, k_scr, v1_scr = rest
    seq = k_ref.shape[0]
    nt = (((1,), (1,)), ((), ()))

    @pl.when(pl.program_id(2) == 0)
    def _():
        k = _head_norm(k_ref[...], kg_ref[...])
        if latent:
            k = _rope(k, cosk_ref[...], sink_ref[...])
            k_scr[seq:, :] = ck_ref[...].astype(BF16)
            v1_scr[seq:, 0:HEAD_DIM] = cv_ref[...].astype(BF16)
        else:
            knew_ref[...] = k
        k_scr[0:seq, :] = k.astype(BF16)
        v1_scr[0:seq, 0:HEAD_DIM] = v_ref[...]
        v1_scr[:, HEAD_DIM:] = jnp.ones((v1_scr.shape[0], HEAD_DIM), BF16)

    q_scale = HEAD_DIM ** -0.5 * LOG2E
    for hh in range(KV_REP):
        sl = slice(hh * HEAD_DIM, (hh + 1) * HEAD_DIM)
        q = _head_norm(q_ref[:, sl], qg_ref[...])
        if latent:
            q = _rope(q, cosq_ref[...], sinq_ref[...])
        s = lax.dot_general((q * q_scale).astype(BF16), k_scr[...], nt, preferred_element_type=F32)
        m = jnp.max(s, axis=-1, keepdims=True)
        o = _dot(jnp.exp2(s - m).astype(BF16), v1_scr[...])
        o_ref[:, sl] = (o[:, :HEAD_DIM] / o[:, HEAD_DIM:]).astype(BF16)


def _attention(qk, vb, seq, latent, layer, q_g, k_g, rope_tabs, cache_k4, cache_v4):
    t = qk.shape[0]
    nb = t // seq
    tq = min(512, seq)
    qt = seq // tq
    gw = KV_REP * HEAD_DIM
    const = lambda b, g, i: (0, 0)
    in_specs = [
        pl.BlockSpec((tq, gw), lambda b, g, i: (b * qt + i, g)),
        pl.BlockSpec((seq, HEAD_DIM), lambda b, g, i: (b, N_HEADS + g)),
        pl.BlockSpec((seq, HEAD_DIM), lambda b, g, i: (b, g)),
        pl.BlockSpec((1, HEAD_DIM), const), pl.BlockSpec((1, HEAD_DIM), const),
    ]
    args = [qk, qk, vb, q_g, k_g]
    past = cache_k4.shape[2] if latent else 0
    out_shape = [jax.ShapeDtypeStruct((t, N_HEADS * HEAD_DIM), BF16)]
    out_specs = [pl.BlockSpec((tq, gw), lambda b, g, i: (b * qt + i, g))]
    if latent:
        tab_q = pl.BlockSpec((tq, HEAD_DIM), lambda b, g, i: (i, 0))
        tab_k = pl.BlockSpec((seq, HEAD_DIM), const)
        cspec = pl.BlockSpec((None, None, past, HEAD_DIM), lambda b, g, i: (b, layer, 0, g))
        in_specs += [tab_q, tab_q, tab_k, tab_k, cspec, cspec]
        args += [rope_tabs[0], rope_tabs[1], rope_tabs[0], rope_tabs[1], cache_k4, cache_v4]
    else:
        out_shape.append(jax.ShapeDtypeStruct((t, KV_W), F32))
        out_specs.append(pl.BlockSpec((seq, HEAD_DIM), lambda b, g, i: (b, g)))
    return pl.pallas_call(
        functools.partial(_attn_kernel, latent),
        out_shape=out_shape,
        grid=(nb, N_KV_HEADS, qt),
        in_specs=in_specs,
        out_specs=out_specs,
        scratch_shapes=[pltpu.VMEM((seq + past, HEAD_DIM), BF16), pltpu.VMEM((seq + past, 2 * HEAD_DIM), BF16)],
        compiler_params=_cparams(("parallel", "parallel", "arbitrary")),
        name="attention_lat" if latent else "attention_ctx",
    )(*args)


def _conv3(p, prev_row, next_row, w, first, last):
    tm = p.shape[0]
    sub = 8
    rid = lax.broadcasted_iota(jnp.int32, (sub, p.shape[1]), 0)
    prev_row = jnp.where(first, 0.0, prev_row)
    next_row = jnp.where(last, 0.0, next_row)
    dn = pltpu.roll(p, 1, 0)
    up = pltpu.roll(p, tm - 1, 0)
    dn = jnp.concatenate([jnp.where(rid == 0, prev_row, dn[:sub, :]), dn[sub:, :]], axis=0)
    up = jnp.concatenate([up[:tm - sub, :], jnp.where(rid == sub - 1, next_row, up[tm - sub:, :])], axis=0)
    return w[0:1, :] * dn + w[1:2, :] * p + w[2:3, :] * up


GW = SSD_HPG * SSD_HEADDIM
NBC = N_SSD_GROUPS * D_STATE
CONV_SLAB = 1024


def _ssd_chunk(rev, xcols, dtr_ref, dtrt_ref, brow_ref, bcol_ref, arow_ref, acol_ref, st_scr, d_ref, emit):
    ii = lax.broadcasted_iota(jnp.int32, (CHUNK, CHUNK), 0)
    jj = lax.broadcasted_iota(jnp.int32, (CHUNK, CHUNK), 1)
    keep = (ii <= jj) if rev else (ii >= jj)
    tri = jnp.where(keep, 1.0, 0.0).astype(BF16)
    tri_t = jnp.where((jj <= ii) if rev else (jj >= ii), 1.0, 0.0).astype(BF16)

    dt = _softplus(dtr_ref[...] + brow_ref[...])
    acs = sum(_dot(tri, part) for part in _split3(dt * (-jnp.exp(arow_ref[...]))))
    dt_t = _softplus(dtrt_ref[...] + bcol_ref[...])
    acs_t = sum(_dot(part, tri_t) for part in _split3(dt_t * (-jnp.exp(acol_ref[...]))))

    edge = 0 if rev else CHUNK - 1
    ea = jnp.exp(acs)
    dec_row = ea[edge:edge + 1, :]
    w_t = dt_t * jnp.exp(acs_t[:, edge:edge + 1] - acs_t)
    acs2 = acs * LOG2E
    l2_t = acs_t * LOG2E - jnp.log2(dt_t)
    head_of_lane = lax.broadcasted_iota(jnp.int32, (CHUNK, GW), 1) // SSD_HEADDIM
    head_of_lane_row = lax.broadcasted_iota(jnp.int32, (1, GW), 1) // SSD_HEADDIM
    lane0 = N_SSD_HEADS if rev else 0
    zero_blk = jnp.zeros((D_STATE, CHUNK), BF16)

    for g in range(N_SSD_GROUPS):
        gs = slice(g * GW, (g + 1) * GW)
        xs = xcols(g * GW, (g + 1) * GW)
        bc = xcols(D_INNER + g * D_STATE, D_INNER + (g + 1) * D_STATE)
        cc = xcols(D_INNER + NBC + g * D_STATE, D_INNER + NBC + (g + 1) * D_STATE)
        bc_t = bc.astype(F32).T
        cc_f = cc.astype(F32)
        cb = _dot(cc, bc_t.astype(BF16))
        st = st_scr[:, gs]
        st_b = st.astype(BF16)
        acc = None
        dec_w = jnp.zeros((1, GW), F32)
        for r in range(SSD_HPG):
            ln = lane0 + g * SSD_HPG + r
            seg = acs2[:, ln:ln + 1] - l2_t[ln:ln + 1, :]
            m = cb * jnp.exp2(jnp.where(keep, seg, -jnp.inf))
            cce = cc_f * ea[:, ln:ln + 1]
            bw = bc_t * w_t[ln:ln + 1, :]
            lhs = jnp.concatenate([jnp.concatenate([m.astype(BF16), cce.astype(BF16)], axis=1),
                                   jnp.concatenate([bw.astype(BF16), zero_blk], axis=1)], axis=0)
            sel = head_of_lane == r
            rhs = jnp.concatenate([jnp.where(sel, xs, jnp.zeros_like(xs)),
                                   jnp.where(sel, st_b, jnp.zeros_like(st_b))], axis=0)
            part = _dot(lhs, rhs)
            acc = part if acc is None else acc + part
            dec_w = jnp.where(head_of_lane_row == r, dec_row[:, ln:ln + 1], dec_w)
        y = acc[:CHUNK, :]
        if not rev:
            y = y + d_ref[:, gs] * xs.astype(F32)
        emit(g, y)
        st_scr[:, gs] = st * dec_w + acc[CHUNK:, :]


def _ssd_kernel(nchunks, has_init, emit_final, xbc_ref, xp_ref, xn_ref, cw_ref, cb_ref, dtr_ref, dtrt_ref,
                brow_ref, bcol_ref, arow_ref, acol_ref, d_ref, *rest):
    rest = list(rest)
    s0_ref = rest.pop(0) if has_init else None
    y_ref = rest.pop(0)
    sfin_ref = rest.pop(0) if emit_final else None
    xact_scr, yf_scr, st_scr = rest
    s = pl.program_id(1)
    scan_args = (dtr_ref, dtrt_ref, brow_ref, bcol_ref, arow_ref, acol_ref, st_scr, d_ref)

    def load_state(d):
        if not has_init:
            st_scr[...] = jnp.zeros_like(st_scr)
            return
        for k in range(D_INNER // LANES):
            sl = slice(k * LANES, (k + 1) * LANES)
            st_scr[:, sl] = s0_ref[d, sl, :].T

    def store_state(d):
        if not emit_final:
            return
        for k in range(D_INNER // LANES):
            sl = slice(k * LANES, (k + 1) * LANES)
            sfin_ref[d, sl, :] = st_scr[:, sl].T

    @pl.when(s == 0)
    def _():
        load_state(0)

    @pl.when(s < nchunks)
    def _():
        c = s
        r = BF16_ROWS - 1
        for k in range(XBC_W // CONV_SLAB):
            sl = slice(k * CONV_SLAB, (k + 1) * CONV_SLAB)
            conv = _conv3(xbc_ref[:, sl].astype(F32), xp_ref[r:r + 1, sl].astype(F32),
                          xn_ref[0:1, sl].astype(F32), cw_ref[:, sl], c == 0, c == nchunks - 1)
            xact_scr[c, :, sl] = _silu(conv + cb_ref[:, sl]).astype(BF16)

        def emit(g, y):
            yf_scr[c, :, g * GW:(g + 1) * GW] = y.astype(BF16)

        _ssd_chunk(False, lambda lo, hi: xact_scr[c, :, lo:hi], *scan_args, emit)

    @pl.when(s == nchunks - 1)
    def _():
        store_state(0)

    @pl.when(s == nchunks)
    def _():
        load_state(1)

    @pl.when(s >= nchunks)
    def _():
        c = 2 * nchunks - 1 - s

        def emit(g, y):
            gs = slice(g * GW, (g + 1) * GW)
            y_ref[:, gs] = (y + yf_scr[c, :, gs].astype(F32)).astype(BF16)

        _ssd_chunk(True, lambda lo, hi: xact_scr[c, :, lo:hi], *scan_args, emit)

    @pl.when(s == 2 * nchunks - 1)
    def _():
        store_state(1)


def _ssd(proj, dt_raw, dt_raw_t, seq, conv_w, conv_b, bias, a_log, d_wide, layer, s0, emit_final):
    t = proj.shape[0]
    nb = t // seq
    nchunks = seq // CHUNK
    rpc = CHUNK // BF16_ROWS
    last_blk = t // BF16_ROWS - 1
    xb = C_XBC // XBC_W

    def chunk_in(b, s):
        return b * nchunks + jnp.minimum(s, nchunks - 1)

    def chunk_of(b, s):
        return b * nchunks + jnp.where(s < nchunks, s, 2 * nchunks - 1 - s)

    def chunk_out(b, s):
        return b * nchunks + jnp.where(s < nchunks, nchunks - 1, 2 * nchunks - 1 - s)

    pad = jnp.zeros((LANES - 2 * N_SSD_HEADS,), F32)
    b_row = jnp.concatenate([bias.reshape(-1), pad]).reshape(1, LANES)
    a_row = jnp.concatenate([a_log.reshape(-1), pad]).reshape(1, LANES)
    const = lambda b, s: (0, 0)
    in_specs = [
        pl.BlockSpec((CHUNK, XBC_W), lambda b, s: (chunk_in(b, s), xb)),
        pl.BlockSpec((BF16_ROWS, XBC_W), lambda b, s: (jnp.maximum(chunk_in(b, s) * rpc - 1, 0), xb)),
        pl.BlockSpec((BF16_ROWS, XBC_W), lambda b, s: (jnp.minimum((chunk_in(b, s) + 1) * rpc, last_blk), xb)),
        pl.BlockSpec((CONV_K, XBC_W), const),
        pl.BlockSpec((1, XBC_W), const),
        pl.BlockSpec((CHUNK, LANES), lambda b, s: (chunk_of(b, s), 0)),
        pl.BlockSpec((LANES, CHUNK), lambda b, s: (0, chunk_of(b, s))),
        pl.BlockSpec((1, LANES), const), pl.BlockSpec((LANES, 1), const),
        pl.BlockSpec((1, LANES), const), pl.BlockSpec((LANES, 1), const),
        pl.BlockSpec((1, D_INNER), const),
    ]
    args = [proj, proj, proj, conv_w, conv_b, dt_raw, dt_raw_t, b_row, b_row.reshape(LANES, 1), a_row,
            a_row.reshape(LANES, 1), d_wide]
    out_shape = [jax.ShapeDtypeStruct((t, D_INNER), BF16)]
    out_specs = [pl.BlockSpec((CHUNK, D_INNER), lambda b, s: (chunk_out(b, s), 0))]
    if s0 is not None:
        in_specs.append(pl.BlockSpec((None, None, 2, D_INNER, D_STATE), lambda b, s: (b, layer, 0, 0, 0)))
        args.append(s0)
    if emit_final:
        out_shape.append(jax.ShapeDtypeStruct((nb, 2, D_INNER, D_STATE), F32))
        out_specs.append(pl.BlockSpec((None, 2, D_INNER, D_STATE), lambda b, s: (b, 0, 0, 0)))
    return pl.pallas_call(
        functools.partial(_ssd_kernel, nchunks, s0 is not None, emit_final),
        out_shape=out_shape,
        grid=(nb, 2 * nchunks),
        in_specs=in_specs,
        out_specs=out_specs,
        scratch_shapes=[pltpu.VMEM((nchunks, CHUNK, XBC_W), BF16), pltpu.VMEM((nchunks, CHUNK, D_INNER), BF16),
                        pltpu.VMEM((D_STATE, D_INNER), F32)],
        compiler_params=_cparams(("parallel", "arbitrary")),
        name="ssd_lat" if s0 is not None else "ssd_ctx",
    )(*args)


def _merge_kernel(tm, seq, x_ref, gate_ref, attn_ref, bg_ref, cg_ref, cx_ref, cgp_ref, cxp_ref, cgn_ref, cxn_ref,
                  cw_ref, y_ref, z_ref, gl_ref, ng_ref, wa_ref, wc_ref, ws_ref, wm_ref, o_ref):
    i = pl.program_id(0)
    tps = seq // tm
    r = BF16_ROWS - 1
    p = cg_ref[...].astype(F32) * cx_ref[...].astype(F32)
    prev_row = cgp_ref[r:r + 1, :].astype(F32) * cxp_ref[r:r + 1, :].astype(F32)
    next_row = cgn_ref[0:1, :].astype(F32) * cxn_ref[0:1, :].astype(F32)
    conv = _conv3(p, prev_row, next_row, cw_ref[...], (i % tps) == 0, (i % tps) == tps - 1)
    u = (bg_ref[...].astype(F32) * conv).astype(BF16)
    tg = y_ref[...].astype(F32) * z_ref[...].astype(F32)
    ms = jnp.mean(tg * tg, axis=-1, keepdims=True)
    yn = (tg * lax.rsqrt(ms + EPS) * ng_ref[...]).astype(BF16)
    y_ssd = _dot(yn, ws_ref[...])
    y_attn = _dot(attn_ref[...], wa_ref[...])
    y_conv = _dot(u, wc_ref[...])
    d = D_MODEL
    mix = (gl_ref[:, 0:d].astype(F32) * y_attn + gl_ref[:, d:2 * d].astype(F32) * y_conv
           + gl_ref[:, 2 * d:3 * d].astype(F32) * y_ssd)
    o_ref[...] = x_ref[...] + gate_ref[...] * _dot(mix.astype(BF16), wm_ref[...])


def _resident(shape):
    return pl.BlockSpec(shape, lambda *_: (0,) * len(shape), pipeline_mode=pl.Buffered(1))


def _merge(x, mod4, layer, seq, latent, proj, attn, y, conv_w, norm_g, wa, wc, ws, wm):
    t = x.shape[0]
    tm = 256
    tps = seq // tm
    row_fn = (lambda i: LAT_ROW0 + i // tps) if latent else (lambda i: CTX_ROW)
    d = D_MODEL
    rpt = tm // BF16_ROWS
    last_blk = t // BF16_ROWS - 1

    def tile(width, col_block):
        return pl.BlockSpec((tm, width), lambda i: (i, col_block))

    def prev(cb):
        return pl.BlockSpec((BF16_ROWS, d), lambda i: (jnp.maximum(i * rpt - 1, 0), cb))

    def nxt(cb):
        return pl.BlockSpec((BF16_ROWS, d), lambda i: (jnp.minimum((i + 1) * rpt, last_blk), cb))

    return pl.pallas_call(
        functools.partial(_merge_kernel, tm, seq),
        out_shape=jax.ShapeDtypeStruct((t, d), F32),
        grid=(t // tm,),
        in_specs=[
            tile(d, 0),
            _mod_spec(layer, 2, row_fn),
            tile(d, 0),
            tile(d, C_BG // d), tile(d, C_CG // d), tile(d, C_CX // d),
            prev(C_CG // d), prev(C_CX // d), nxt(C_CG // d), nxt(C_CX // d),
            _resident((CONV_K, d)),
            tile(D_INNER, 0),
            tile(D_INNER, C_Z // D_INNER),
            tile(3 * d, C_GATE // (3 * d)),
            _resident((1, D_INNER)),
            _resident((d, d)), _resident((d, d)), _resident((D_INNER, d)), _resident((d, d)),
        ],
        out_specs=tile(d, 0),
        compiler_params=_cparams(("parallel",)),
        name="merge",
    )(x, mod4, attn, proj, proj, proj, proj, proj, proj, proj, conv_w, y, proj, proj, norm_g, wa, wc, ws, wm)


FF_SPLIT = 2


def _ffn_kernel(x_ref, shift_ref, scale_ref, gate_ref, g_ref, w1_ref, w2_ref, o_ref):
    x = x_ref[...]
    h = _norm_mod(x, g_ref[...], scale_ref[...], shift_ref[...]).astype(BF16)
    slab = D_FF // FF_SPLIT
    acc = None
    for j in range(FF_SPLIT):
        hg = _dot(h, w1_ref[:, j * slab:(j + 1) * slab])
        hu = _dot(h, w1_ref[:, D_FF + j * slab:D_FF + (j + 1) * slab])
        part = _dot((_silu(hg) * hu).astype(BF16), w2_ref[j * slab:(j + 1) * slab, :])
        acc = part if acc is None else acc + part
    o_ref[...] = x + gate_ref[...] * acc


def _ffn(x, mod4, layer, seq, latent, norm_g, w1, w2):
    t = x.shape[0]
    tm = 512
    tps = max(seq // tm, 1)
    row_fn = (lambda i: LAT_ROW0 + i // tps) if latent else (lambda i: CTX_ROW)
    return pl.pallas_call(
        _ffn_kernel,
        out_shape=jax.ShapeDtypeStruct((t, D_MODEL), F32),
        grid=(t // tm,),
        in_specs=[
            pl.BlockSpec((tm, D_MODEL), lambda i: (i, 0)),
            _mod_spec(layer, 3, row_fn),
            _mod_spec(layer, 4, row_fn),
            _mod_spec(layer, 5, row_fn),
            _resident((1, D_MODEL)),
            _resident((D_MODEL, 2 * D_FF)),
            _resident((D_FF, D_MODEL)),
        ],
        out_specs=pl.BlockSpec((tm, D_MODEL), lambda i: (i, 0)),
        compiler_params=_cparams(("parallel",)),
        name="ffn",
    )(x, mod4, mod4, mod4, norm_g, w1, w2)


def _rope_tables(n_tok):
    rows = n_tok // GRID_W
    row = jnp.repeat(jnp.arange(rows, dtype=F32), GRID_W)
    col = jnp.tile(jnp.arange(GRID_W, dtype=F32), rows)
    axis_dim = HEAD_DIM // 2
    inv_freq = 1.0 / (ROPE_THETA ** (jnp.arange(0, axis_dim, 2, dtype=F32) / axis_dim))
    ang = jnp.concatenate([row[:, None] * inv_freq, col[:, None] * inv_freq], axis=-1)
    cos = jnp.repeat(jnp.cos(ang), 2, axis=-1)
    sin = jnp.repeat(jnp.sin(ang), 2, axis=-1)
    sign = jnp.tile(jnp.asarray([-1.0, 1.0], F32), HEAD_DIM // 2)
    return cos, sin * sign


def _layer(x, seq, latent, layer, mod4, wts, cache_k4, cache_v4, s0, rope_tabs):
    proj, dt_raw, qk, vb, *v32 = _in_proj(x, mod4, layer, seq, latent, wts["norm1_g"], wts["w_main"], wts["w_dt"])
    attn, *k32 = _attention(qk, vb, seq, latent, layer, wts["q_g"], wts["k_g"], rope_tabs, cache_k4, cache_v4)
    kv = k32 + v32
    y, *s_fin = _ssd(proj, dt_raw, dt_raw.T, seq, wts["ssd_conv_w"], wts["ssd_conv_b"], wts["dt_bias"],
                     wts["a_log"], wts["d_wide"], layer, s0, not latent)
    x = _merge(x, mod4, layer, seq, latent, proj, attn, y, wts["conv_w"], wts["ssd_norm_g"],
               wts["w_attn_o"], wts["w_conv_o"], wts["w_ssd_o"], wts["w_merge"])
    x = _ffn(x, mod4, layer, seq, latent, wts["norm2_g"], wts["ffn_w1"], wts["ffn_w2"])
    return x, kv, s_fin


def _layer_weights(l, norm1_g, norm2_g, w_in, q_norm_g, k_norm_g, w_attn_o, conv_w, w_conv_o, ssd_conv_w,
                   ssd_conv_b, ssd_dt_bias, ssd_a_log, ssd_d, ssd_norm_g, w_ssd_o, w_merge, ffn_w1, ffn_w2):
    d = D_MODEL
    kvw = N_KV_HEADS * HEAD_DIM
    o_q, o_k, o_v, o_bg, o_cg, o_cx = 0, d, d + kvw, d + 2 * kvw, 2 * d + 2 * kvw, 3 * d + 2 * kvw
    o_z = o_cx + d
    o_xbc = o_z + D_INNER
    o_dt = o_xbc + XBC_W
    o_gate = o_dt + 2 * N_SSD_HEADS
    w = w_in[l]
    cols = lambda a, n: w[:, a:a + n]
    w_main = jnp.concatenate([cols(o_xbc, XBC_W), cols(o_z, D_INNER), cols(o_gate, 3 * d), cols(o_bg, d),
                              cols(o_cg, d), cols(o_cx, d), cols(o_q, d), cols(o_k, kvw), cols(o_v, kvw)],
                             axis=1).astype(BF16)
    w_dt = jnp.concatenate([cols(o_dt, 2 * N_SSD_HEADS), jnp.zeros((d, LANES - 2 * N_SSD_HEADS), F32)],
                           axis=1).astype(BF16)
    return {
        "norm1_g": norm1_g[l].reshape(1, d), "norm2_g": norm2_g[l].reshape(1, d),
        "w_main": w_main, "w_dt": w_dt,
        "q_g": q_norm_g[l].reshape(1, HEAD_DIM), "k_g": k_norm_g[l].reshape(1, HEAD_DIM),
        "w_attn_o": w_attn_o[l].astype(BF16), "conv_w": conv_w[l], "w_conv_o": w_conv_o[l].astype(BF16),
        "ssd_conv_w": ssd_conv_w[l], "ssd_conv_b": ssd_conv_b[l].reshape(1, -1),
        "dt_bias": ssd_dt_bias[l], "a_log": ssd_a_log[l],
        "d_wide": jnp.repeat(ssd_d[l], SSD_HEADDIM).reshape(1, D_INNER),
        "ssd_norm_g": ssd_norm_g[l].reshape(1, D_INNER), "w_ssd_o": w_ssd_o[l].astype(BF16),
        "w_merge": w_merge[l].astype(BF16), "ffn_w1": ffn_w1[l].astype(BF16), "ffn_w2": ffn_w2[l].astype(BF16),
    }


def kernel(x_prompt, x_sample, c, cache_k, cache_v, state_ssd, c_ctx, ada_w, ada_b, norm1_g, norm2_g, w_in,
           q_norm_g, k_norm_g, w_attn_o, conv_w, w_conv_o, ssd_conv_w, ssd_conv_b, ssd_dt_bias, ssd_a_log, ssd_d,
           ssd_norm_g, w_ssd_o, w_merge, ffn_w1, ffn_w2):
    nb_ctx, seq_ctx, d = x_prompt.shape
    nb_lat, seq_lat, _ = x_sample.shape
    past = cache_k.shape[2]
    cond = jnp.concatenate([c_ctx[None, :], c, jnp.zeros((MOD_ROWS - nb_lat - 1, d), F32)], axis=0)
    mod4 = _modulation(cond, ada_w, ada_b).reshape(DEPTH, MOD_ROWS, 1, 6 * d)
    cache_k4 = cache_k.reshape(nb_lat, DEPTH, past, N_KV_HEADS * HEAD_DIM)
    cache_v4 = cache_v.reshape(nb_lat, DEPTH, past, N_KV_HEADS * HEAD_DIM)
    rope_tabs = _rope_tables(seq_lat)
    s0_lat = state_ssd.reshape(nb_lat, DEPTH, 2, D_INNER, D_STATE)

    y_ctx = x_prompt.reshape(nb_ctx * seq_ctx, d)
    y_lat = x_sample.reshape(nb_lat * seq_lat, d)
    ks, vs, ss = [], [], []
    for l in range(DEPTH):
        wts = _layer_weights(l, norm1_g, norm2_g, w_in, q_norm_g, k_norm_g, w_attn_o, conv_w, w_conv_o,
                             ssd_conv_w, ssd_conv_b, ssd_dt_bias, ssd_a_log, ssd_d, ssd_norm_g, w_ssd_o,
                             w_merge, ffn_w1, ffn_w2)
        y_ctx, (k32, v32), (s_fin,) = _layer(y_ctx, seq_ctx, False, l, mod4, wts, None, None, None, None)
        y_lat, _, _ = _layer(y_lat, seq_lat, True, l, mod4, wts, cache_k4, cache_v4, s0_lat, rope_tabs)
        ks.append(k32.reshape(nb_ctx, seq_ctx, N_KV_HEADS, HEAD_DIM))
        vs.append(v32.reshape(nb_ctx, seq_ctx, N_KV_HEADS, HEAD_DIM))
        ss.append(s_fin.reshape(nb_ctx, 2, N_SSD_HEADS, SSD_HEADDIM, D_STATE))
    return (y_ctx.reshape(nb_ctx, seq_ctx, d), y_lat.reshape(nb_lat, seq_lat, d),
            jnp.stack(ks, axis=1), jnp.stack(vs, axis=1), jnp.stack(ss, axis=1))
```

```python
import functools

import jax
import jax.numpy as jnp
from jax import lax
from jax.experimental import pallas as pl
from jax.experimental.pallas import tpu as pltpu

F32 = jnp.float32
BF16 = jnp.bfloat16

D_MODEL = 1024
DEPTH = 2
GRID_W = 64
N_HEADS = 8
N_KV_HEADS = 2
HEAD_DIM = 128
KV_REP = N_HEADS // N_KV_HEADS
ROPE_THETA = 10000.0
CONV_K = 3
D_INNER = 2 * D_MODEL
SSD_HEADDIM = 64
N_SSD_HEADS = D_INNER // SSD_HEADDIM
N_SSD_GROUPS = 8
SSD_HPG = N_SSD_HEADS // N_SSD_GROUPS
D_STATE = 128
CHUNK = 128
D_FF = -(-8 * D_MODEL // 768) * 256
EPS = 1e-6

LANES = 128
BF16_ROWS = 16
VMEM_LIMIT = 56 * 1024 * 1024

XBC_W = 2 * D_INNER
C_XBC = 0
C_Z = C_XBC + XBC_W
C_GATE = C_Z + D_INNER
C_BG = C_GATE + 3 * D_MODEL
C_CG = C_BG + D_MODEL
C_CX = C_CG + D_MODEL
C_Q = C_CX + D_MODEL
C_K = C_Q + D_MODEL
C_V = C_K + N_KV_HEADS * HEAD_DIM
N_MAIN = C_V + N_KV_HEADS * HEAD_DIM
PROJ_TN = N_MAIN - C_Q
J_QKV = C_Q // PROJ_TN
LOG2E = 1.4426950408889634
MOD_ROWS = 16
CTX_ROW = 0
LAT_ROW0 = 1


def _cparams(sem):
    return pltpu.CompilerParams(dimension_semantics=sem, vmem_limit_bytes=VMEM_LIMIT)


def _dot(a, b):
    return jnp.dot(a, b, preferred_element_type=F32)


def _sigmoid(x):
    return 1.0 / (1.0 + jnp.exp(-x))


def _silu(x):
    return x * _sigmoid(x)


def _softplus(x):
    return jnp.maximum(x, 0.0) + jnp.log1p(jnp.exp(-jnp.abs(x)))


def _split3(x):
    hi = x.astype(BF16)
    r1 = x - hi.astype(F32)
    mid = r1.astype(BF16)
    lo = (r1 - mid.astype(F32)).astype(BF16)
    return hi, mid, lo


def _mod_kernel(c_ref, w_ref, b_ref, o_ref):
    s = _silu(c_ref[...]).astype(BF16)
    o_ref[...] = _dot(s, w_ref[...].astype(BF16)) + b_ref[...]


def _modulation(cond, ada_w, ada_b):
    tn = 1536
    return pl.pallas_call(
        _mod_kernel,
        out_shape=jax.ShapeDtypeStruct((DEPTH, MOD_ROWS, 6 * D_MODEL), F32),
        grid=(DEPTH, 6 * D_MODEL // tn),
        in_specs=[
            pl.BlockSpec((MOD_ROWS, D_MODEL), lambda l, j: (0, 0)),
            pl.BlockSpec((None, D_MODEL, tn), lambda l, j: (l, 0, j)),
            pl.BlockSpec((None, 1, tn), lambda l, j: (l, 0, j)),
        ],
        out_specs=pl.BlockSpec((None, MOD_ROWS, tn), lambda l, j: (l, 0, j)),
        compiler_params=_cparams(("parallel", "parallel")),
        name="modulation",
    )(cond, ada_w, ada_b.reshape(DEPTH, 1, 6 * D_MODEL))


def _mod_spec(layer, which, row_fn):
    return pl.BlockSpec((None, None, 1, D_MODEL), lambda *ids: (layer, row_fn(*ids), 0, which))


def _norm_mod(x, g, scale, shift):
    ms = jnp.mean(x * x, axis=-1, keepdims=True)
    return (x * lax.rsqrt(ms + EPS) * g) * (1.0 + scale) + shift


def _head_norm(a, g):
    ms = jnp.mean(a * a, axis=-1, keepdims=True)
    return a * lax.rsqrt(ms + EPS) * g


def _rope(a, cos, sin_signed):
    lane = lax.broadcasted_iota(jnp.int32, a.shape, 1)
    nxt = pltpu.roll(a, HEAD_DIM - 1, 1)
    prv = pltpu.roll(a, 1, 1)
    swapped = jnp.where(lane % 2 == 0, nxt, prv)
    return a * cos + swapped * sin_signed


def _inproj_kernel(latent, x_ref, shift_ref, scale_ref, g_ref, w_ref, wdt_ref, proj_ref, dt_ref, qk_ref, vb_ref,
                   *rest):
    if latent:
        (h_scr,) = rest
    else:
        v32_ref, h_scr = rest
    j = pl.program_id(1)

    @pl.when(j == 0)
    def _():
        h = _norm_mod(x_ref[...], g_ref[...], scale_ref[...], shift_ref[...]).astype(BF16)
        h_scr[...] = h
        dt_ref[...] = _dot(h, wdt_ref[...])

    def activation(col):
        if C_Z <= col < C_GATE:
            return _silu
        if C_GATE <= col < C_BG:
            return _sigmoid
        return None

    for jt in range(J_QKV):
        c0, c1 = jt * PROJ_TN, (jt + 1) * PROJ_TN
        cuts = [c0] + [c for c in (C_Z, C_GATE, C_BG) if c0 < c < c1] + [c1]

        @pl.when(j == jt)
        def _(c0=c0, cuts=cuts):
            for a, b in zip(cuts[:-1], cuts[1:]):
                acc = _dot(h_scr[...], w_ref[:, a - c0:b - c0])
                fn = activation(a)
                proj_ref[:, a - c0:b - c0] = (acc if fn is None else fn(acc)).astype(BF16)

    @pl.when(j == J_QKV)
    def _():
        acc = _dot(h_scr[...], w_ref[...])
        v0 = C_V - C_Q
        qk_ref[...] = acc[:, :v0]
        if not latent:
            v32_ref[...] = acc[:, v0:]
        vb_ref[...] = acc[:, v0:].astype(BF16)


QK_W = C_V - C_Q
KV_W = N_KV_HEADS * HEAD_DIM


def _in_proj(x, mod4, layer, seq, latent, norm_g, w_main, w_dt):
    t = x.shape[0]
    tm = 1024
    tiles_per_seq = max(seq // tm, 1)
    row_fn = (lambda i, j: LAT_ROW0 + i // tiles_per_seq) if latent else (lambda i, j: CTX_ROW)
    once = lambda width: pl.BlockSpec((tm, width), lambda i, j: (i, 0))
    in_specs = [
        pl.BlockSpec((tm, D_MODEL), lambda i, j: (i, 0)),
        _mod_spec(layer, 0, row_fn),
        _mod_spec(layer, 1, row_fn),
        pl.BlockSpec((1, D_MODEL), lambda i, j: (0, 0)),
        pl.BlockSpec((D_MODEL, PROJ_TN), lambda i, j: (0, j)),
        pl.BlockSpec((D_MODEL, LANES), lambda i, j: (0, 0)),
    ]
    out_shape = [jax.ShapeDtypeStruct((t, C_Q), BF16), jax.ShapeDtypeStruct((t, LANES), F32),
                 jax.ShapeDtypeStruct((t, QK_W), F32), jax.ShapeDtypeStruct((t, KV_W), BF16)]
    out_specs = [pl.BlockSpec((tm, PROJ_TN), lambda i, j: (i, jnp.minimum(j, J_QKV - 1))),
                 once(LANES), once(QK_W), once(KV_W)]
    if not latent:
        out_shape.append(jax.ShapeDtypeStruct((t, KV_W), F32))
        out_specs.append(once(KV_W))
    return pl.pallas_call(
        functools.partial(_inproj_kernel, latent),
        out_shape=out_shape,
        grid=(t // tm, N_MAIN // PROJ_TN),
        in_specs=in_specs,
        out_specs=out_specs,
        scratch_shapes=[pltpu.VMEM((tm, D_MODEL), BF16)],
        compiler_params=_cparams(("parallel", "arbitrary")),
        name="in_proj_lat" if latent else "in_proj_ctx",
    )(x, mod4, mod4, norm_g, w_main, w_dt)


def _attn_kernel(latent, q_ref, k_ref, v_ref, qg_ref, kg_ref, *rest):
    if latent:
        cosq_ref, sinq_ref, cosk_ref, sink_ref, ck_ref, cv_ref, o_ref, k_scr, v1_scr = rest
    else:
        o_ref, knew_ref, k_scr, v1_scr = rest
    seq = k_ref.shape[0]
    nt = (((1,), (1,)), ((), ()))

    @pl.when(pl.program_id(2) == 0)
    def _():
        k = _head_norm(k_ref[...], kg_ref[...])
        if latent:
            k = _rope(k, cosk_ref[...], sink_ref[...])
            k_scr[seq:, :] = ck_ref[...].astype(BF16)
            v1_scr[seq:, 0:HEAD_DIM] = cv_ref[...].astype(BF16)
        else:
            knew_ref[...] = k
        k_scr[0:seq, :] = k.astype(BF16)
        v1_scr[0:seq, 0:HEAD_DIM] = v_ref[...]
        v1_scr[:, HEAD_DIM:] = jnp.ones((v1_scr.shape[0], HEAD_DIM), BF16)

    q_scale = HEAD_DIM ** -0.5 * LOG2E
    for hh in range(KV_REP):
        sl = slice(hh * HEAD_DIM, (hh + 1) * HEAD_DIM)
        q = _head_norm(q_ref[:, sl], qg_ref[...])
        if latent:
            q = _rope(q, cosq_ref[...], sinq_ref[...])
        s = lax.dot_general((q * q_scale).astype(BF16), k_scr[...], nt, preferred_element_type=F32)
        m = jnp.max(s, axis=-1, keepdims=True)
        o = _dot(jnp.exp2(s - m).astype(BF16), v1_scr[...])
        o_ref[:, sl] = (o[:, :HEAD_DIM] / o[:, HEAD_DIM:]).astype(BF16)


def _attention(qk, vb, seq, latent, layer, q_g, k_g, rope_tabs, cache_k4, cache_v4):
    t = qk.shape[0]
    nb = t // seq
    tq = min(512, seq)
    qt = seq // tq
    gw = KV_REP * HEAD_DIM
    const = lambda b, g, i: (0, 0)
    in_specs = [
        pl.BlockSpec((tq, gw), lambda b, g, i: (b * qt + i, g)),
        pl.BlockSpec((seq, HEAD_DIM), lambda b, g, i: (b, N_HEADS + g)),
        pl.BlockSpec((seq, HEAD_DIM), lambda b, g, i: (b, g)),
        pl.BlockSpec((1, HEAD_DIM), const), pl.BlockSpec((1, HEAD_DIM), const),
    ]
    args = [qk, qk, vb, q_g, k_g]
    past = cache_k4.shape[2] if latent else 0
    out_shape = [jax.ShapeDtypeStruct((t, N_HEADS * HEAD_DIM), BF16)]
    out_specs = [pl.BlockSpec((tq, gw), lambda b, g, i: (b * qt + i, g))]
    if latent:
        tab_q = pl.BlockSpec((tq, HEAD_DIM), lambda b, g, i: (i, 0))
        tab_k = pl.BlockSpec((seq, HEAD_DIM), const)
        cspec = pl.BlockSpec((None, None, past, HEAD_DIM), lambda b, g, i: (b, layer, 0, g))
        in_specs += [tab_q, tab_q, tab_k, tab_k, cspec, cspec]
        args += [rope_tabs[0], rope_tabs[1], rope_tabs[0], rope_tabs[1], cache_k4, cache_v4]
    else:
        out_shape.append(jax.ShapeDtypeStruct((t, KV_W), F32))
        out_specs.append(pl.BlockSpec((seq, HEAD_DIM), lambda b, g, i: (b, g)))
    return pl.pallas_call(
        functools.partial(_attn_kernel, latent),
        out_shape=out_shape,
        grid=(nb, N_KV_HEADS, qt),
        in_specs=in_specs,
        out_specs=out_specs,
        scratch_shapes=[pltpu.VMEM((seq + past, HEAD_DIM), BF16), pltpu.VMEM((seq + past, 2 * HEAD_DIM), BF16)],
        compiler_params=_cparams(("parallel", "parallel", "arbitrary")),
        name="attention_lat" if latent else "attention_ctx",
    )(*args)


def _conv3(p, prev_row, next_row, w, first, last):
    tm = p.shape[0]
    sub = 8
    rid = lax.broadcasted_iota(jnp.int32, (sub, p.shape[1]), 0)
    prev_row = jnp.where(first, 0.0, prev_row)
    next_row = jnp.where(last, 0.0, next_row)
    dn = pltpu.roll(p, 1, 0)
    up = pltpu.roll(p, tm - 1, 0)
    dn = jnp.concatenate([jnp.where(rid == 0, prev_row, dn[:sub, :]), dn[sub:, :]], axis=0)
    up = jnp.concatenate([up[:tm - sub, :], jnp.where(rid == sub - 1, next_row, up[tm - sub:, :])], axis=0)
    return w[0:1, :] * dn + w[1:2, :] * p + w[2:3, :] * up


GW = SSD_HPG * SSD_HEADDIM
NBC = N_SSD_GROUPS * D_STATE
CONV_SLAB = 1024


def _ssd_chunk(rev, xcols, dtr, dtr_t, brow_ref, bcol_ref, arow_ref, acol_ref, st_scr, d_ref, emit):
    ii = lax.broadcasted_iota(jnp.int32, (CHUNK, CHUNK), 0)
    jj = lax.broadcasted_iota(jnp.int32, (CHUNK, CHUNK), 1)
    keep = (ii <= jj) if rev else (ii >= jj)
    tri = jnp.where(keep, 1.0, 0.0).astype(BF16)
    tri_t = jnp.where((jj <= ii) if rev else (jj >= ii), 1.0, 0.0).astype(BF16)

    dt = _softplus(dtr + brow_ref[...])
    acs = sum(_dot(tri, part) for part in _split3(dt * (-jnp.exp(arow_ref[...]))))
    dt_t = _softplus(dtr_t + bcol_ref[...])
    acs_t = sum(_dot(part, tri_t) for part in _split3(dt_t * (-jnp.exp(acol_ref[...]))))

    edge = 0 if rev else CHUNK - 1
    ea = jnp.exp(acs)
    dec_row = ea[edge:edge + 1, :]
    w_t = dt_t * jnp.exp(acs_t[:, edge:edge + 1] - acs_t)
    acs2 = acs * LOG2E
    l2_t = acs_t * LOG2E - jnp.log2(dt_t)
    head_of_lane = lax.broadcasted_iota(jnp.int32, (CHUNK, GW), 1) // SSD_HEADDIM
    head_of_lane_row = lax.broadcasted_iota(jnp.int32, (1, GW), 1) // SSD_HEADDIM
    lane0 = N_SSD_HEADS if rev else 0
    zero_blk = jnp.zeros((D_STATE, CHUNK), BF16)

    for g in range(N_SSD_GROUPS):
        gs = slice(g * GW, (g + 1) * GW)
        xs = xcols(g * GW, (g + 1) * GW)
        bc = xcols(D_INNER + g * D_STATE, D_INNER + (g + 1) * D_STATE)
        cc = xcols(D_INNER + NBC + g * D_STATE, D_INNER + NBC + (g + 1) * D_STATE)
        bc_t = bc.astype(F32).T
        cc_f = cc.astype(F32)
        cb = _dot(cc, bc_t.astype(BF16))
        st = st_scr[:, gs]
        st_b = st.astype(BF16)
        acc = None
        dec_w = jnp.zeros((1, GW), F32)
        for r in range(SSD_HPG):
            ln = lane0 + g * SSD_HPG + r
            seg = acs2[:, ln:ln + 1] - l2_t[ln:ln + 1, :]
            m = cb * jnp.exp2(jnp.where(keep, seg, -jnp.inf))
            cce = cc_f * ea[:, ln:ln + 1]
            bw = bc_t * w_t[ln:ln + 1, :]
            lhs = jnp.concatenate([jnp.concatenate([m.astype(BF16), cce.astype(BF16)], axis=1),
                                   jnp.concatenate([bw.astype(BF16), zero_blk], axis=1)], axis=0)
            sel = head_of_lane == r
            rhs = jnp.concatenate([jnp.where(sel, xs, jnp.zeros_like(xs)),
                                   jnp.where(sel, st_b, jnp.zeros_like(st_b))], axis=0)
            part = _dot(lhs, rhs)
            acc = part if acc is None else acc + part
            dec_w = jnp.where(head_of_lane_row == r, dec_row[:, ln:ln + 1], dec_w)
        y = acc[:CHUNK, :]
        if not rev:
            y = y + d_ref[:, gs] * xs.astype(F32)
        emit(g, y)
        st_scr[:, gs] = st * dec_w + acc[CHUNK:, :]


SSD_MAX_CPS = 4


def _ssd_kernel(nsteps, cps, has_init, emit_final, xbc_ref, xp_ref, xn_ref, cw_ref, cb_ref, dtr_ref, dtrt_ref,
                brow_ref, bcol_ref, arow_ref, acol_ref, d_ref, *rest):
    rest = list(rest)
    s0_ref = rest.pop(0) if has_init else None
    y_ref = rest.pop(0)
    sfin_ref = rest.pop(0) if emit_final else None
    xact_scr, yf_scr, st_scr = rest
    s = pl.program_id(1)
    scan_args = (brow_ref, bcol_ref, arow_ref, acol_ref, st_scr, d_ref)

    def scan(rev, blk, k, emit):
        rows = slice(k * CHUNK, (k + 1) * CHUNK)
        _ssd_chunk(rev, lambda lo, hi: xact_scr[blk, rows, lo:hi], dtr_ref[rows, :], dtrt_ref[:, rows],
                   *scan_args, functools.partial(emit, rows))

    def load_state(d):
        if not has_init:
            st_scr[...] = jnp.zeros_like(st_scr)
            return
        for k in range(D_INNER // LANES):
            sl = slice(k * LANES, (k + 1) * LANES)
            st_scr[:, sl] = s0_ref[d, sl, :].T

    def store_state(d):
        if not emit_final:
            return
        for k in range(D_INNER // LANES):
            sl = slice(k * LANES, (k + 1) * LANES)
            sfin_ref[d, sl, :] = st_scr[:, sl].T

    @pl.when(s == 0)
    def _():
        load_state(0)

    @pl.when(s < nsteps)
    def _():
        blk = s
        r = BF16_ROWS - 1
        for k in range(XBC_W // CONV_SLAB):
            sl = slice(k * CONV_SLAB, (k + 1) * CONV_SLAB)
            conv = _conv3(xbc_ref[:, sl].astype(F32), xp_ref[r:r + 1, sl].astype(F32),
                          xn_ref[0:1, sl].astype(F32), cw_ref[:, sl], s == 0, s == nsteps - 1)
            xact_scr[blk, :, sl] = _silu(conv + cb_ref[:, sl]).astype(BF16)

        def emit(rows, g, y):
            yf_scr[blk, rows, g * GW:(g + 1) * GW] = y.astype(BF16)

        for k in range(cps):
            scan(False, blk, k, emit)

    @pl.when(s == nsteps - 1)
    def _():
        store_state(0)

    @pl.when(s == nsteps)
    def _():
        load_state(1)

    @pl.when(s >= nsteps)
    def _():
        blk = 2 * nsteps - 1 - s

        def emit(rows, g, y):
            gs = slice(g * GW, (g + 1) * GW)
            y_ref[rows, gs] = (y + yf_scr[blk, rows, gs].astype(F32)).astype(BF16)

        for k in reversed(range(cps)):
            scan(True, blk, k, emit)

    @pl.when(s == 2 * nsteps - 1)
    def _():
        store_state(1)


def _ssd(proj, dt_raw, dt_raw_t, seq, conv_w, conv_b, bias, a_log, d_wide, layer, s0, emit_final):
    t = proj.shape[0]
    nb = t // seq
    cps = min(SSD_MAX_CPS, seq // CHUNK)
    rows = cps * CHUNK
    nchunks = nsteps = seq // rows
    rpc = rows // BF16_ROWS
    last_blk = t // BF16_ROWS - 1
    xb = C_XBC // XBC_W

    def chunk_in(b, s):
        return b * nchunks + jnp.minimum(s, nchunks - 1)

    def chunk_of(b, s):
        return b * nchunks + jnp.where(s < nchunks, s, 2 * nchunks - 1 - s)

    def chunk_out(b, s):
        return b * nchunks + jnp.where(s < nchunks, nchunks - 1, 2 * nchunks - 1 - s)

    pad = jnp.zeros((LANES - 2 * N_SSD_HEADS,), F32)
    b_row = jnp.concatenate([bias.reshape(-1), pad]).reshape(1, LANES)
    a_row = jnp.concatenate([a_log.reshape(-1), pad]).reshape(1, LANES)
    const = lambda b, s: (0, 0)
    in_specs = [
        pl.BlockSpec((rows, XBC_W), lambda b, s: (chunk_in(b, s), xb)),
        pl.BlockSpec((BF16_ROWS, XBC_W), lambda b, s: (jnp.maximum(chunk_in(b, s) * rpc - 1, 0), xb)),
        pl.BlockSpec((BF16_ROWS, XBC_W), lambda b, s: (jnp.minimum((chunk_in(b, s) + 1) * rpc, last_blk), xb)),
        pl.BlockSpec((CONV_K, XBC_W), const),
        pl.BlockSpec((1, XBC_W), const),
        pl.BlockSpec((rows, LANES), lambda b, s: (chunk_of(b, s), 0)),
        pl.BlockSpec((LANES, rows), lambda b, s: (0, chunk_of(b, s))),
        pl.BlockSpec((1, LANES), const), pl.BlockSpec((LANES, 1), const),
        pl.BlockSpec((1, LANES), const), pl.BlockSpec((LANES, 1), const),
        pl.BlockSpec((1, D_INNER), const),
    ]
    args = [proj, proj, proj, conv_w, conv_b, dt_raw, dt_raw_t, b_row, b_row.reshape(LANES, 1), a_row,
            a_row.reshape(LANES, 1), d_wide]
    out_shape = [jax.ShapeDtypeStruct((t, D_INNER), BF16)]
    out_specs = [pl.BlockSpec((rows, D_INNER), lambda b, s: (chunk_out(b, s), 0))]
    if s0 is not None:
        in_specs.append(pl.BlockSpec((None, None, 2, D_INNER, D_STATE), lambda b, s: (b, layer, 0, 0, 0)))
        args.append(s0)
    if emit_final:
        out_shape.append(jax.ShapeDtypeStruct((nb, 2, D_INNER, D_STATE), F32))
        out_specs.append(pl.BlockSpec((None, 2, D_INNER, D_STATE), lambda b, s: (b, 0, 0, 0)))
    return pl.pallas_call(
        functools.partial(_ssd_kernel, nsteps, cps, s0 is not None, emit_final),
        out_shape=out_shape,
        grid=(nb, 2 * nsteps),
        in_specs=in_specs,
        out_specs=out_specs,
        scratch_shapes=[pltpu.VMEM((nsteps, rows, XBC_W), BF16), pltpu.VMEM((nsteps, rows, D_INNER), BF16),
                        pltpu.VMEM((D_STATE, D_INNER), F32)],
        compiler_params=_cparams(("parallel", "arbitrary")),
        name="ssd_lat" if s0 is not None else "ssd_ctx",
    )(*args)


def _merge_kernel(tm, seq, x_ref, gate_ref, attn_ref, bg_ref, cg_ref, cx_ref, cgp_ref, cxp_ref, cgn_ref, cxn_ref,
                  cw_ref, y_ref, z_ref, gl_ref, ng_ref, wa_ref, wc_ref, ws_ref, wm_ref, o_ref):
    i = pl.program_id(0)
    tps = seq // tm
    r = BF16_ROWS - 1
    p = cg_ref[...].astype(F32) * cx_ref[...].astype(F32)
    prev_row = cgp_ref[r:r + 1, :].astype(F32) * cxp_ref[r:r + 1, :].astype(F32)
    next_row = cgn_ref[0:1, :].astype(F32) * cxn_ref[0:1, :].astype(F32)
    conv = _conv3(p, prev_row, next_row, cw_ref[...], (i % tps) == 0, (i % tps) == tps - 1)
    u = (bg_ref[...].astype(F32) * conv).astype(BF16)
    tg = y_ref[...].astype(F32) * z_ref[...].astype(F32)
    ms = jnp.mean(tg * tg, axis=-1, keepdims=True)
    yn = (tg * lax.rsqrt(ms + EPS) * ng_ref[...]).astype(BF16)
    y_ssd = _dot(yn, ws_ref[...])
    y_attn = _dot(attn_ref[...], wa_ref[...])
    y_conv = _dot(u, wc_ref[...])
    d = D_MODEL
    mix = (gl_ref[:, 0:d].astype(F32) * y_attn + gl_ref[:, d:2 * d].astype(F32) * y_conv
           + gl_ref[:, 2 * d:3 * d].astype(F32) * y_ssd)
    o_ref[...] = x_ref[...] + gate_ref[...] * _dot(mix.astype(BF16), wm_ref[...])


def _resident(shape):
    return pl.BlockSpec(shape, lambda *_: (0,) * len(shape), pipeline_mode=pl.Buffered(1))


def _merge(x, mod4, layer, seq, latent, proj, attn, y, conv_w, norm_g, wa, wc, ws, wm):
    t = x.shape[0]
    tm = 256
    tps = seq // tm
    row_fn = (lambda i: LAT_ROW0 + i // tps) if latent else (lambda i: CTX_ROW)
    d = D_MODEL
    rpt = tm // BF16_ROWS
    last_blk = t // BF16_ROWS - 1

    def tile(width, col_block):
        return pl.BlockSpec((tm, width), lambda i: (i, col_block))

    def prev(cb):
        return pl.BlockSpec((BF16_ROWS, d), lambda i: (jnp.maximum(i * rpt - 1, 0), cb))

    def nxt(cb):
        return pl.BlockSpec((BF16_ROWS, d), lambda i: (jnp.minimum((i + 1) * rpt, last_blk), cb))

    return pl.pallas_call(
        functools.partial(_merge_kernel, tm, seq),
        out_shape=jax.ShapeDtypeStruct((t, d), F32),
        grid=(t // tm,),
        in_specs=[
            tile(d, 0),
            _mod_spec(layer, 2, row_fn),
            tile(d, 0),
            tile(d, C_BG // d), tile(d, C_CG // d), tile(d, C_CX // d),
            prev(C_CG // d), prev(C_CX // d), nxt(C_CG // d), nxt(C_CX // d),
            _resident((CONV_K, d)),
            tile(D_INNER, 0),
            tile(D_INNER, C_Z // D_INNER),
            tile(3 * d, C_GATE // (3 * d)),
            _resident((1, D_INNER)),
            _resident((d, d)), _resident((d, d)), _resident((D_INNER, d)), _resident((d, d)),
        ],
        out_specs=tile(d, 0),
        compiler_params=_cparams(("parallel",)),
        name="merge",
    )(x, mod4, attn, proj, proj, proj, proj, proj, proj, proj, conv_w, y, proj, proj, norm_g, wa, wc, ws, wm)


FF_SPLIT = 2


def _ffn_kernel(x_ref, shift_ref, scale_ref, gate_ref, g_ref, w1_ref, w2_ref, o_ref):
    x = x_ref[...]
    h = _norm_mod(x, g_ref[...], scale_ref[...], shift_ref[...]).astype(BF16)
    slab = D_FF // FF_SPLIT
    acc = None
    for j in range(FF_SPLIT):
        hg = _dot(h, w1_ref[:, j * slab:(j + 1) * slab])
        hu = _dot(h, w1_ref[:, D_FF + j * slab:D_FF + (j + 1) * slab])
        part = _dot((_silu(hg) * hu).astype(BF16), w2_ref[j * slab:(j + 1) * slab, :])
        acc = part if acc is None else acc + part
    o_ref[...] = x + gate_ref[...] * acc


def _ffn(x, mod4, layer, seq, latent, norm_g, w1, w2):
    t = x.shape[0]
    tm = 512
    tps = max(seq // tm, 1)
    row_fn = (lambda i: LAT_ROW0 + i // tps) if latent else (lambda i: CTX_ROW)
    return pl.pallas_call(
        _ffn_kernel,
        out_shape=jax.ShapeDtypeStruct((t, D_MODEL), F32),
        grid=(t // tm,),
        in_specs=[
            pl.BlockSpec((tm, D_MODEL), lambda i: (i, 0)),
            _mod_spec(layer, 3, row_fn),
            _mod_spec(layer, 4, row_fn),
            _mod_spec(layer, 5, row_fn),
            _resident((1, D_MODEL)),
            _resident((D_MODEL, 2 * D_FF)),
            _resident((D_FF, D_MODEL)),
        ],
        out_specs=pl.BlockSpec((tm, D_MODEL), lambda i: (i, 0)),
        compiler_params=_cparams(("parallel",)),
        name="ffn",
    )(x, mod4, mod4, mod4, norm_g, w1, w2)


def _rope_tables(n_tok):
    rows = n_tok // GRID_W
    row = jnp.repeat(jnp.arange(rows, dtype=F32), GRID_W)
    col = jnp.tile(jnp.arange(GRID_W, dtype=F32), rows)
    axis_dim = HEAD_DIM // 2
    inv_freq = 1.0 / (ROPE_THETA ** (jnp.arange(0, axis_dim, 2, dtype=F32) / axis_dim))
    ang = jnp.concatenate([row[:, None] * inv_freq, col[:, None] * inv_freq], axis=-1)
    cos = jnp.repeat(jnp.cos(ang), 2, axis=-1)
    sin = jnp.repeat(jnp.sin(ang), 2, axis=-1)
    sign = jnp.tile(jnp.asarray([-1.0, 1.0], F32), HEAD_DIM // 2)
    return cos, sin * sign


def _layer(x, seq, latent, layer, mod4, wts, cache_k4, cache_v4, s0, rope_tabs):
    proj, dt_raw, qk, vb, *v32 = _in_proj(x, mod4, layer, seq, latent, wts["norm1_g"], wts["w_main"], wts["w_dt"])
    attn, *k32 = _attention(qk, vb, seq, latent, layer, wts["q_g"], wts["k_g"], rope_tabs, cache_k4, cache_v4)
    kv = k32 + v32
    y, *s_fin = _ssd(proj, dt_raw, dt_raw.T, seq, wts["ssd_conv_w"], wts["ssd_conv_b"], wts["dt_bias"],
                     wts["a_log"], wts["d_wide"], layer, s0, not latent)
    x = _merge(x, mod4, layer, seq, latent, proj, attn, y, wts["conv_w"], wts["ssd_norm_g"],
               wts["w_attn_o"], wts["w_conv_o"], wts["w_ssd_o"], wts["w_merge"])
    x = _ffn(x, mod4, layer, seq, latent, wts["norm2_g"], wts["ffn_w1"], wts["ffn_w2"])
    return x, kv, s_fin


def _layer_weights(l, norm1_g, norm2_g, w_in, q_norm_g, k_norm_g, w_attn_o, conv_w, w_conv_o, ssd_conv_w,
                   ssd_conv_b, ssd_dt_bias, ssd_a_log, ssd_d, ssd_norm_g, w_ssd_o, w_merge, ffn_w1, ffn_w2):
    d = D_MODEL
    kvw = N_KV_HEADS * HEAD_DIM
    o_q, o_k, o_v, o_bg, o_cg, o_cx = 0, d, d + kvw, d + 2 * kvw, 2 * d + 2 * kvw, 3 * d + 2 * kvw
    o_z = o_cx + d
    o_xbc = o_z + D_INNER
    o_dt = o_xbc + XBC_W
    o_gate = o_dt + 2 * N_SSD_HEADS
    w = w_in[l]
    cols = lambda a, n: w[:, a:a + n]
    w_main = jnp.concatenate([cols(o_xbc, XBC_W), cols(o_z, D_INNER), cols(o_gate, 3 * d), cols(o_bg, d),
                              cols(o_cg, d), cols(o_cx, d), cols(o_q, d), cols(o_k, kvw), cols(o_v, kvw)],
                             axis=1).astype(BF16)
    w_dt = jnp.concatenate([cols(o_dt, 2 * N_SSD_HEADS), jnp.zeros((d, LANES - 2 * N_SSD_HEADS), F32)],
                           axis=1).astype(BF16)
    return {
        "norm1_g": norm1_g[l].reshape(1, d), "norm2_g": norm2_g[l].reshape(1, d),
        "w_main": w_main, "w_dt": w_dt,
        "q_g": q_norm_g[l].reshape(1, HEAD_DIM), "k_g": k_norm_g[l].reshape(1, HEAD_DIM),
        "w_attn_o": w_attn_o[l].astype(BF16), "conv_w": conv_w[l], "w_conv_o": w_conv_o[l].astype(BF16),
        "ssd_conv_w": ssd_conv_w[l], "ssd_conv_b": ssd_conv_b[l].reshape(1, -1),
        "dt_bias": ssd_dt_bias[l], "a_log": ssd_a_log[l],
        "d_wide": jnp.repeat(ssd_d[l], SSD_HEADDIM).reshape(1, D_INNER),
        "ssd_norm_g": ssd_norm_g[l].reshape(1, D_INNER), "w_ssd_o": w_ssd_o[l].astype(BF16),
        "w_merge": w_merge[l].astype(BF16), "ffn_w1": ffn_w1[l].astype(BF16), "ffn_w2": ffn_w2[l].astype(BF16),
    }


def kernel(x_prompt, x_sample, c, cache_k, cache_v, state_ssd, c_ctx, ada_w, ada_b, norm1_g, norm2_g, w_in,
           q_norm_g, k_norm_g, w_attn_o, conv_w, w_conv_o, ssd_conv_w, ssd_conv_b, ssd_dt_bias, ssd_a_log, ssd_d,
           ssd_norm_g, w_ssd_o, w_merge, ffn_w1, ffn_w2):
    nb_ctx, seq_ctx, d = x_prompt.shape
    nb_lat, seq_lat, _ = x_sample.shape
    past = cache_k.shape[2]
    cond = jnp.concatenate([c_ctx[None, :], c, jnp.zeros((MOD_ROWS - nb_lat - 1, d), F32)], axis=0)
    mod4 = _modulation(cond, ada_w, ada_b).reshape(DEPTH, MOD_ROWS, 1, 6 * d)
    cache_k4 = cache_k.reshape(nb_lat, DEPTH, past, N_KV_HEADS * HEAD_DIM)
    cache_v4 = cache_v.reshape(nb_lat, DEPTH, past, N_KV_HEADS * HEAD_DIM)
    rope_tabs = _rope_tables(seq_lat)
    s0_lat = state_ssd.reshape(nb_lat, DEPTH, 2, D_INNER, D_STATE)

    y_ctx = x_prompt.reshape(nb_ctx * seq_ctx, d)
    y_lat = x_sample.reshape(nb_lat * seq_lat, d)
    ks, vs, ss = [], [], []
    for l in range(DEPTH):
        wts = _layer_weights(l, norm1_g, norm2_g, w_in, q_norm_g, k_norm_g, w_attn_o, conv_w, w_conv_o,
                             ssd_conv_w, ssd_conv_b, ssd_dt_bias, ssd_a_log, ssd_d, ssd_norm_g, w_ssd_o,
                             w_merge, ffn_w1, ffn_w2)
        y_ctx, (k32, v32), (s_fin,) = _layer(y_ctx, seq_ctx, False, l, mod4, wts, None, None, None, None)
        y_lat, _, _ = _layer(y_lat, seq_lat, True, l, mod4, wts, cache_k4, cache_v4, s0_lat, rope_tabs)
        ks.append(k32.reshape(nb_ctx, seq_ctx, N_KV_HEADS, HEAD_DIM))
        vs.append(v32.reshape(nb_ctx, seq_ctx, N_KV_HEADS, HEAD_DIM))
        ss.append(s_fin.reshape(nb_ctx, 2, N_SSD_HEADS, SSD_HEADDIM, D_STATE))
    return (y_ctx.reshape(nb_ctx, seq_ctx, d), y_lat.reshape(nb_lat, seq_lat, d),
            jnp.stack(ks, axis=1), jnp.stack(vs, axis=1), jnp.stack(ss, axis=1))
```

```python
import functools

import jax
import jax.numpy as jnp
from jax import lax
from jax.experimental import pallas as pl
from jax.experimental.pallas import tpu as pltpu

F32 = jnp.float32
BF16 = jnp.bfloat16

D_MODEL = 1024
DEPTH = 2
GRID_W = 64
N_HEADS = 8
N_KV_HEADS = 2
HEAD_DIM = 128
KV_REP = N_HEADS // N_KV_HEADS
ROPE_THETA = 10000.0
CONV_K = 3
D_INNER = 2 * D_MODEL
SSD_HEADDIM = 64
N_SSD_HEADS = D_INNER // SSD_HEADDIM
N_SSD_GROUPS = 8
SSD_HPG = N_SSD_HEADS // N_SSD_GROUPS
D_STATE = 128
CHUNK = 128
D_FF = -(-8 * D_MODEL // 768) * 256
EPS = 1e-6

LANES = 128
BF16_ROWS = 16
VMEM_LIMIT = 56 * 1024 * 1024

XBC_W = 2 * D_INNER
C_XBC = 0
C_Z = C_XBC + XBC_W
C_GATE = C_Z + D_INNER
C_BG = C_GATE + 3 * D_MODEL
C_CG = C_BG + D_MODEL
C_CX = C_CG + D_MODEL
C_Q = C_CX + D_MODEL
C_K = C_Q + D_MODEL
C_V = C_K + N_KV_HEADS * HEAD_DIM
N_MAIN = C_V + N_KV_HEADS * HEAD_DIM
PROJ_TN = N_MAIN - C_Q
J_QKV = C_Q // PROJ_TN
LOG2E = 1.4426950408889634
MOD_ROWS = 16
CTX_ROW = 0
LAT_ROW0 = 1


def _cparams(sem):
    return pltpu.CompilerParams(dimension_semantics=sem, vmem_limit_bytes=VMEM_LIMIT)


def _dot(a, b):
    return jnp.dot(a, b, preferred_element_type=F32)


def _sigmoid(x):
    return 1.0 / (1.0 + jnp.exp(-x))


def _silu(x):
    return x * _sigmoid(x)


def _softplus(x):
    return jnp.maximum(x, 0.0) + jnp.log1p(jnp.exp(-jnp.abs(x)))


def _split3(x):
    hi = x.astype(BF16)
    r1 = x - hi.astype(F32)
    mid = r1.astype(BF16)
    lo = (r1 - mid.astype(F32)).astype(BF16)
    return hi, mid, lo


def _mod_kernel(c_ref, w_ref, b_ref, o_ref):
    s = _silu(c_ref[...]).astype(BF16)
    o_ref[...] = _dot(s, w_ref[...].astype(BF16)) + b_ref[...]


def _modulation(cond, ada_w, ada_b):
    tn = 1536
    return pl.pallas_call(
        _mod_kernel,
        out_shape=jax.ShapeDtypeStruct((DEPTH, MOD_ROWS, 6 * D_MODEL), F32),
        grid=(DEPTH, 6 * D_MODEL // tn),
        in_specs=[
            pl.BlockSpec((MOD_ROWS, D_MODEL), lambda l, j: (0, 0)),
            pl.BlockSpec((None, D_MODEL, tn), lambda l, j: (l, 0, j)),
            pl.BlockSpec((None, 1, tn), lambda l, j: (l, 0, j)),
        ],
        out_specs=pl.BlockSpec((None, MOD_ROWS, tn), lambda l, j: (l, 0, j)),
        compiler_params=_cparams(("parallel", "parallel")),
        name="modulation",
    )(cond, ada_w, ada_b.reshape(DEPTH, 1, 6 * D_MODEL))


def _mod_spec(layer, which, row_fn):
    return pl.BlockSpec((None, None, 1, D_MODEL), lambda *ids: (layer, row_fn(*ids), 0, which))


def _norm_mod(x, g, scale, shift):
    ms = jnp.mean(x * x, axis=-1, keepdims=True)
    return (x * lax.rsqrt(ms + EPS) * g) * (1.0 + scale) + shift


def _head_norm(a, g):
    ms = jnp.mean(a * a, axis=-1, keepdims=True)
    return a * lax.rsqrt(ms + EPS) * g


def _rope(a, cos, sin_signed):
    lane = lax.broadcasted_iota(jnp.int32, a.shape, 1)
    nxt = pltpu.roll(a, HEAD_DIM - 1, 1)
    prv = pltpu.roll(a, 1, 1)
    swapped = jnp.where(lane % 2 == 0, nxt, prv)
    return a * cos + swapped * sin_signed


def _inproj_kernel(latent, x_ref, shift_ref, scale_ref, g_ref, w_ref, wdt_ref, proj_ref, dt_ref, qk_ref, vb_ref,
                   *rest):
    if latent:
        (h_scr,) = rest
    else:
        v32_ref, h_scr = rest
    j = pl.program_id(1)

    @pl.when(j == 0)
    def _():
        h = _norm_mod(x_ref[...], g_ref[...], scale_ref[...], shift_ref[...]).astype(BF16)
        h_scr[...] = h
        dt_ref[...] = _dot(h, wdt_ref[...])

    def activation(col):
        if C_Z <= col < C_GATE:
            return _silu
        if C_GATE <= col < C_BG:
            return _sigmoid
        return None

    for jt in range(J_QKV):
        c0, c1 = jt * PROJ_TN, (jt + 1) * PROJ_TN
        cuts = [c0] + [c for c in (C_Z, C_GATE, C_BG) if c0 < c < c1] + [c1]

        @pl.when(j == jt)
        def _(c0=c0, cuts=cuts):
            for a, b in zip(cuts[:-1], cuts[1:]):
                acc = _dot(h_scr[...], w_ref[:, a - c0:b - c0])
                fn = activation(a)
                proj_ref[:, a - c0:b - c0] = (acc if fn is None else fn(acc)).astype(BF16)

    @pl.when(j == J_QKV)
    def _():
        acc = _dot(h_scr[...], w_ref[...])
        v0 = C_V - C_Q
        qk_ref[...] = acc[:, :v0]
        if not latent:
            v32_ref[...] = acc[:, v0:]
        vb_ref[...] = acc[:, v0:].astype(BF16)


QK_W = C_V - C_Q
KV_W = N_KV_HEADS * HEAD_DIM


def _in_proj(x, mod4, layer, seq, latent, norm_g, w_main, w_dt):
    t = x.shape[0]
    tm = 1024
    tiles_per_seq = max(seq // tm, 1)
    row_fn = (lambda i, j: LAT_ROW0 + i // tiles_per_seq) if latent else (lambda i, j: CTX_ROW)
    once = lambda width: pl.BlockSpec((tm, width), lambda i, j: (i, 0))
    in_specs = [
        pl.BlockSpec((tm, D_MODEL), lambda i, j: (i, 0)),
        _mod_spec(layer, 0, row_fn),
        _mod_spec(layer, 1, row_fn),
        pl.BlockSpec((1, D_MODEL), lambda i, j: (0, 0)),
        pl.BlockSpec((D_MODEL, PROJ_TN), lambda i, j: (0, j)),
        pl.BlockSpec((D_MODEL, LANES), lambda i, j: (0, 0)),
    ]
    out_shape = [jax.ShapeDtypeStruct((t, C_Q), BF16), jax.ShapeDtypeStruct((t, LANES), F32),
                 jax.ShapeDtypeStruct((t, QK_W), F32), jax.ShapeDtypeStruct((t, KV_W), BF16)]
    out_specs = [pl.BlockSpec((tm, PROJ_TN), lambda i, j: (i, jnp.minimum(j, J_QKV - 1))),
                 once(LANES), once(QK_W), once(KV_W)]
    if not latent:
        out_shape.append(jax.ShapeDtypeStruct((t, KV_W), F32))
        out_specs.append(once(KV_W))
    return pl.pallas_call(
        functools.partial(_inproj_kernel, latent),
        out_shape=out_shape,
        grid=(t // tm, N_MAIN // PROJ_TN),
        in_specs=in_specs,
        out_specs=out_specs,
        scratch_shapes=[pltpu.VMEM((tm, D_MODEL), BF16)],
        compiler_params=_cparams(("parallel", "arbitrary")),
        name="in_proj_lat" if latent else "in_proj_ctx",
    )(x, mod4, mod4, norm_g, w_main, w_dt)


def _attn_kernel(latent, q_ref, k_ref, v_ref, qg_ref, kg_ref, *rest):
    if latent:
        cosq_ref, sinq_ref, cosk_ref, sink_ref, ck_ref, cv_ref, o_ref, k_scr, v1_scr = rest
    else:
        o_ref, knew_ref, k_scr, v1_scr = rest
    seq = k_ref.shape[0]
    nt = (((1,), (1,)), ((), ()))

    @pl.when(pl.program_id(2) == 0)
    def _():
        k = _head_norm(k_ref[...], kg_ref[...])
        if latent:
            k = _rope(k, cosk_ref[...], sink_ref[...])
            k_scr[seq:, :] = ck_ref[...].astype(BF16)
            v1_scr[seq:, 0:HEAD_DIM] = cv_ref[...].astype(BF16)
        else:
            knew_ref[...] = k
        k_scr[0:seq, :] = k.astype(BF16)
        v1_scr[0:seq, 0:HEAD_DIM] = v_ref[...]
        v1_scr[:, HEAD_DIM:] = jnp.ones((v1_scr.shape[0], HEAD_DIM), BF16)

    q_scale = HEAD_DIM ** -0.5 * LOG2E
    for hh in range(KV_REP):
        sl = slice(hh * HEAD_DIM, (hh + 1) * HEAD_DIM)
        q = _head_norm(q_ref[:, sl], qg_ref[...])
        if latent:
            q = _rope(q, cosq_ref[...], sinq_ref[...])
        s = lax.dot_general((q * q_scale).astype(BF16), k_scr[...], nt, preferred_element_type=F32)
        m = jnp.max(s, axis=-1, keepdims=True)
        o = _dot(jnp.exp2(s - m).astype(BF16), v1_scr[...])
        o_ref[:, sl] = (o[:, :HEAD_DIM] / o[:, HEAD_DIM:]).astype(BF16)


def _attention(qk, vb, seq, latent, layer, q_g, k_g, rope_tabs, cache_k4, cache_v4):
    t = qk.shape[0]
    nb = t // seq
    tq = min(512, seq)
    qt = seq // tq
    gw = KV_REP * HEAD_DIM
    const = lambda b, g, i: (0, 0)
    in_specs = [
        pl.BlockSpec((tq, gw), lambda b, g, i: (b * qt + i, g)),
        pl.BlockSpec((seq, HEAD_DIM), lambda b, g, i: (b, N_HEADS + g)),
        pl.BlockSpec((seq, HEAD_DIM), lambda b, g, i: (b, g)),
        pl.BlockSpec((1, HEAD_DIM), const), pl.BlockSpec((1, HEAD_DIM), const),
    ]
    args = [qk, qk, vb, q_g, k_g]
    past = cache_k4.shape[2] if latent else 0
    out_shape = [jax.ShapeDtypeStruct((t, N_HEADS * HEAD_DIM), BF16)]
    out_specs = [pl.BlockSpec((tq, gw), lambda b, g, i: (b * qt + i, g))]
    if latent:
        tab_q = pl.BlockSpec((tq, HEAD_DIM), lambda b, g, i: (i, 0))
        tab_k = pl.BlockSpec((seq, HEAD_DIM), const)
        cspec = pl.BlockSpec((None, None, past, HEAD_DIM), lambda b, g, i: (b, layer, 0, g))
        in_specs += [tab_q, tab_q, tab_k, tab_k, cspec, cspec]
        args += [rope_tabs[0], rope_tabs[1], rope_tabs[0], rope_tabs[1], cache_k4, cache_v4]
    else:
        out_shape.append(jax.ShapeDtypeStruct((t, KV_W), F32))
        out_specs.append(pl.BlockSpec((seq, HEAD_DIM), lambda b, g, i: (b, g)))
    return pl.pallas_call(
        functools.partial(_attn_kernel, latent),
        out_shape=out_shape,
        grid=(nb, N_KV_HEADS, qt),
        in_specs=in_specs,
        out_specs=out_specs,
        scratch_shapes=[pltpu.VMEM((seq + past, HEAD_DIM), BF16), pltpu.VMEM((seq + past, 2 * HEAD_DIM), BF16)],
        compiler_params=_cparams(("parallel", "parallel", "arbitrary")),
        name="attention_lat" if latent else "attention_ctx",
    )(*args)


def _conv3(p, prev_row, next_row, w, first, last):
    tm = p.shape[0]
    sub = 8
    rid = lax.broadcasted_iota(jnp.int32, (sub, p.shape[1]), 0)
    prev_row = jnp.where(first, 0.0, prev_row)
    next_row = jnp.where(last, 0.0, next_row)
    dn = pltpu.roll(p, 1, 0)
    up = pltpu.roll(p, tm - 1, 0)
    dn = jnp.concatenate([jnp.where(rid == 0, prev_row, dn[:sub, :]), dn[sub:, :]], axis=0)
    up = jnp.concatenate([up[:tm - sub, :], jnp.where(rid == sub - 1, next_row, up[tm - sub:, :])], axis=0)
    return w[0:1, :] * dn + w[1:2, :] * p + w[2:3, :] * up


GW = SSD_HPG * SSD_HEADDIM
NBC = N_SSD_GROUPS * D_STATE
CONV_SLAB = 1024


def _ssd_chunk(rev, xcols, dtr, dtr_t, brow_ref, bcol_ref, arow_ref, acol_ref, st_scr, d_ref, emit):
    ii = lax.broadcasted_iota(jnp.int32, (CHUNK, CHUNK), 0)
    jj = lax.broadcasted_iota(jnp.int32, (CHUNK, CHUNK), 1)
    keep = (ii <= jj) if rev else (ii >= jj)
    tri = jnp.where(keep, 1.0, 0.0).astype(BF16)
    tri_t = jnp.where((jj <= ii) if rev else (jj >= ii), 1.0, 0.0).astype(BF16)

    dt = _softplus(dtr + brow_ref[...])
    acs = sum(_dot(tri, part) for part in _split3(dt * (-jnp.exp(arow_ref[...]))))
    dt_t = _softplus(dtr_t + bcol_ref[...])
    acs_t = sum(_dot(part, tri_t) for part in _split3(dt_t * (-jnp.exp(acol_ref[...]))))

    edge = 0 if rev else CHUNK - 1
    ea = jnp.exp(acs)
    dec_row = ea[edge:edge + 1, :]
    w_t = dt_t * jnp.exp(acs_t[:, edge:edge + 1] - acs_t)
    acs2 = acs * LOG2E
    l2_t = acs_t * LOG2E - jnp.log2(dt_t)
    head_of_lane = lax.broadcasted_iota(jnp.int32, (CHUNK, GW), 1) // SSD_HEADDIM
    head_of_lane_row = lax.broadcasted_iota(jnp.int32, (1, GW), 1) // SSD_HEADDIM
    lane0 = N_SSD_HEADS if rev else 0
    zero_blk = jnp.zeros((D_STATE, CHUNK), BF16)

    for g in range(N_SSD_GROUPS):
        gs = slice(g * GW, (g + 1) * GW)
        xs = xcols(g * GW, (g + 1) * GW)
        bc = xcols(D_INNER + g * D_STATE, D_INNER + (g + 1) * D_STATE)
        cc = xcols(D_INNER + NBC + g * D_STATE, D_INNER + NBC + (g + 1) * D_STATE)
        bc_t = bc.astype(F32).T
        cc_f = cc.astype(F32)
        cb = _dot(cc, bc_t.astype(BF16))
        st = st_scr[:, gs]
        st_b = st.astype(BF16)
        acc = None
        dec_w = jnp.zeros((1, GW), F32)
        for r in range(SSD_HPG):
            ln = lane0 + g * SSD_HPG + r
            seg = acs2[:, ln:ln + 1] - l2_t[ln:ln + 1, :]
            m = cb * jnp.exp2(jnp.where(keep, seg, -jnp.inf))
            cce = cc_f * ea[:, ln:ln + 1]
            bw = bc_t * w_t[ln:ln + 1, :]
            lhs = jnp.concatenate([jnp.concatenate([m.astype(BF16), cce.astype(BF16)], axis=1),
                                   jnp.concatenate([bw.astype(BF16), zero_blk], axis=1)], axis=0)
            sel = head_of_lane == r
            rhs = jnp.concatenate([jnp.where(sel, xs, jnp.zeros_like(xs)),
                                   jnp.where(sel, st_b, jnp.zeros_like(st_b))], axis=0)
            part = _dot(lhs, rhs)
            acc = part if acc is None else acc + part
            dec_w = jnp.where(head_of_lane_row == r, dec_row[:, ln:ln + 1], dec_w)
        y = acc[:CHUNK, :]
        if not rev:
            y = y + d_ref[:, gs] * xs.astype(F32)
        emit(g, y)
        st_scr[:, gs] = st * dec_w + acc[CHUNK:, :]


SSD_MAX_CPS = 4


def _ssd_kernel(nsteps, cps, has_init, has_prev, emit_final, xbc_ref, xp_ref, xn_ref, cw_ref, cb_ref, dtr_ref,
                dtrt_ref, brow_ref, bcol_ref, arow_ref, acol_ref, d_ref, *rest):
    rest = list(rest)
    s0_ref = rest.pop(0) if has_init else None
    if has_prev:
        rest.pop(0)
    y_ref = rest.pop(0)
    sfin_ref = rest.pop(0) if emit_final else None
    xact_scr, yf_scr, st_scr = rest
    s = pl.program_id(1)
    scan_args = (brow_ref, bcol_ref, arow_ref, acol_ref, st_scr, d_ref)

    def scan(rev, blk, k, emit):
        rows = slice(k * CHUNK, (k + 1) * CHUNK)
        _ssd_chunk(rev, lambda lo, hi: xact_scr[blk, rows, lo:hi], dtr_ref[rows, :], dtrt_ref[:, rows],
                   *scan_args, functools.partial(emit, rows))

    def load_state(d):
        if not has_init:
            st_scr[...] = jnp.zeros_like(st_scr)
            return
        for k in range(D_INNER // LANES):
            sl = slice(k * LANES, (k + 1) * LANES)
            st_scr[:, sl] = s0_ref[d, sl, :].T

    slab_ref = None
    if emit_final:
        slab_ref = sfin_ref if has_prev else sfin_ref.at[0]

    def store_state(d):
        if not emit_final:
            return
        for k in range(D_INNER // LANES):
            sl = slice(k * LANES, (k + 1) * LANES)
            slab_ref[d, sl, :] = st_scr[:, sl].T

    @pl.when(s == 0)
    def _():
        load_state(0)
        if emit_final and not has_prev:
            sfin_ref[1:] = jnp.zeros((DEPTH - 1,) + tuple(sfin_ref.shape[1:]), F32)

    @pl.when(s < nsteps)
    def _():
        blk = s
        r = BF16_ROWS - 1
        for k in range(XBC_W // CONV_SLAB):
            sl = slice(k * CONV_SLAB, (k + 1) * CONV_SLAB)
            conv = _conv3(xbc_ref[:, sl].astype(F32), xp_ref[r:r + 1, sl].astype(F32),
                          xn_ref[0:1, sl].astype(F32), cw_ref[:, sl], s == 0, s == nsteps - 1)
            xact_scr[blk, :, sl] = _silu(conv + cb_ref[:, sl]).astype(BF16)

        def emit(rows, g, y):
            yf_scr[blk, rows, g * GW:(g + 1) * GW] = y.astype(BF16)

        for k in range(cps):
            scan(False, blk, k, emit)

    @pl.when(s == nsteps - 1)
    def _():
        store_state(0)

    @pl.when(s == nsteps)
    def _():
        load_state(1)

    @pl.when(s >= nsteps)
    def _():
        blk = 2 * nsteps - 1 - s

        def emit(rows, g, y):
            gs = slice(g * GW, (g + 1) * GW)
            y_ref[rows, gs] = (y + yf_scr[blk, rows, gs].astype(F32)).astype(BF16)

        for k in reversed(range(cps)):
            scan(True, blk, k, emit)

    @pl.when(s == 2 * nsteps - 1)
    def _():
        store_state(1)


def _ssd(proj, dt_raw, dt_raw_t, seq, conv_w, conv_b, bias, a_log, d_wide, layer, s0, emit_final, sfin_all):
    t = proj.shape[0]
    nb = t // seq
    cps = min(SSD_MAX_CPS, seq // CHUNK)
    rows = cps * CHUNK
    nchunks = nsteps = seq // rows
    rpc = rows // BF16_ROWS
    last_blk = t // BF16_ROWS - 1
    xb = C_XBC // XBC_W

    def chunk_in(b, s):
        return b * nchunks + jnp.minimum(s, nchunks - 1)

    def chunk_of(b, s):
        return b * nchunks + jnp.where(s < nchunks, s, 2 * nchunks - 1 - s)

    def chunk_out(b, s):
        return b * nchunks + jnp.where(s < nchunks, nchunks - 1, 2 * nchunks - 1 - s)

    pad = jnp.zeros((LANES - 2 * N_SSD_HEADS,), F32)
    b_row = jnp.concatenate([bias.reshape(-1), pad]).reshape(1, LANES)
    a_row = jnp.concatenate([a_log.reshape(-1), pad]).reshape(1, LANES)
    const = lambda b, s: (0, 0)
    in_specs = [
        pl.BlockSpec((rows, XBC_W), lambda b, s: (chunk_in(b, s), xb)),
        pl.BlockSpec((BF16_ROWS, XBC_W), lambda b, s: (jnp.maximum(chunk_in(b, s) * rpc - 1, 0), xb)),
        pl.BlockSpec((BF16_ROWS, XBC_W), lambda b, s: (jnp.minimum((chunk_in(b, s) + 1) * rpc, last_blk), xb)),
        pl.BlockSpec((CONV_K, XBC_W), const),
        pl.BlockSpec((1, XBC_W), const),
        pl.BlockSpec((rows, LANES), lambda b, s: (chunk_of(b, s), 0)),
        pl.BlockSpec((LANES, rows), lambda b, s: (0, chunk_of(b, s))),
        pl.BlockSpec((1, LANES), const), pl.BlockSpec((LANES, 1), const),
        pl.BlockSpec((1, LANES), const), pl.BlockSpec((LANES, 1), const),
        pl.BlockSpec((1, D_INNER), const),
    ]
    args = [proj, proj, proj, conv_w, conv_b, dt_raw, dt_raw_t, b_row, b_row.reshape(LANES, 1), a_row,
            a_row.reshape(LANES, 1), d_wide]
    out_shape = [jax.ShapeDtypeStruct((t, D_INNER), BF16)]
    out_specs = [pl.BlockSpec((rows, D_INNER), lambda b, s: (chunk_out(b, s), 0))]
    state_spec = pl.BlockSpec((None, None, 2, D_INNER, D_STATE), lambda b, s: (b, layer, 0, 0, 0))
    aliases = {}
    if s0 is not None:
        in_specs.append(state_spec)
        args.append(s0)
    if emit_final:
        out_shape.append(jax.ShapeDtypeStruct((nb, DEPTH, 2, D_INNER, D_STATE), F32))
        if sfin_all is not None:
            aliases = {len(args): 1}
            in_specs.append(pl.BlockSpec(memory_space=pl.ANY))
            args.append(sfin_all)
            out_specs.append(state_spec)
        else:
            assert layer == 0
            out_specs.append(pl.BlockSpec((None, DEPTH, 2, D_INNER, D_STATE), lambda b, s: (b, 0, 0, 0, 0)))
    return pl.pallas_call(
        functools.partial(_ssd_kernel, nsteps, cps, s0 is not None, bool(aliases), emit_final),
        out_shape=out_shape,
        grid=(nb, 2 * nsteps),
        in_specs=in_specs,
        out_specs=out_specs,
        input_output_aliases=aliases,
        scratch_shapes=[pltpu.VMEM((nsteps, rows, XBC_W), BF16), pltpu.VMEM((nsteps, rows, D_INNER), BF16),
                        pltpu.VMEM((D_STATE, D_INNER), F32)],
        compiler_params=_cparams(("parallel", "arbitrary")),
        name="ssd_lat" if s0 is not None else "ssd_ctx",
    )(*args)


def _merge_kernel(tm, seq, x_ref, gate_ref, attn_ref, bg_ref, cg_ref, cx_ref, cgp_ref, cxp_ref, cgn_ref, cxn_ref,
                  cw_ref, y_ref, z_ref, gl_ref, ng_ref, wa_ref, wc_ref, ws_ref, wm_ref, o_ref):
    i = pl.program_id(0)
    tps = seq // tm
    r = BF16_ROWS - 1
    p = cg_ref[...].astype(F32) * cx_ref[...].astype(F32)
    prev_row = cgp_ref[r:r + 1, :].astype(F32) * cxp_ref[r:r + 1, :].astype(F32)
    next_row = cgn_ref[0:1, :].astype(F32) * cxn_ref[0:1, :].astype(F32)
    conv = _conv3(p, prev_row, next_row, cw_ref[...], (i % tps) == 0, (i % tps) == tps - 1)
    u = (bg_ref[...].astype(F32) * conv).astype(BF16)
    tg = y_ref[...].astype(F32) * z_ref[...].astype(F32)
    ms = jnp.mean(tg * tg, axis=-1, keepdims=True)
    yn = (tg * lax.rsqrt(ms + EPS) * ng_ref[...]).astype(BF16)
    y_ssd = _dot(yn, ws_ref[...])
    y_attn = _dot(attn_ref[...], wa_ref[...])
    y_conv = _dot(u, wc_ref[...])
    d = D_MODEL
    mix = (gl_ref[:, 0:d].astype(F32) * y_attn + gl_ref[:, d:2 * d].astype(F32) * y_conv
           + gl_ref[:, 2 * d:3 * d].astype(F32) * y_ssd)
    o_ref[...] = x_ref[...] + gate_ref[...] * _dot(mix.astype(BF16), wm_ref[...])


def _resident(shape):
    return pl.BlockSpec(shape, lambda *_: (0,) * len(shape), pipeline_mode=pl.Buffered(1))


def _merge(x, mod4, layer, seq, latent, proj, attn, y, conv_w, norm_g, wa, wc, ws, wm):
    t = x.shape[0]
    tm = min(512, seq)
    tps = seq // tm
    row_fn = (lambda i: LAT_ROW0 + i // tps) if latent else (lambda i: CTX_ROW)
    d = D_MODEL
    rpt = tm // BF16_ROWS
    last_blk = t // BF16_ROWS - 1

    def tile(width, col_block):
        return pl.BlockSpec((tm, width), lambda i: (i, col_block))

    def prev(cb):
        return pl.BlockSpec((BF16_ROWS, d), lambda i: (jnp.maximum(i * rpt - 1, 0), cb))

    def nxt(cb):
        return pl.BlockSpec((BF16_ROWS, d), lambda i: (jnp.minimum((i + 1) * rpt, last_blk), cb))

    return pl.pallas_call(
        functools.partial(_merge_kernel, tm, seq),
        out_shape=jax.ShapeDtypeStruct((t, d), F32),
        grid=(t // tm,),
        in_specs=[
            tile(d, 0),
            _mod_spec(layer, 2, row_fn),
            tile(d, 0),
            tile(d, C_BG // d), tile(d, C_CG // d), tile(d, C_CX // d),
            prev(C_CG // d), prev(C_CX // d), nxt(C_CG // d), nxt(C_CX // d),
            _resident((CONV_K, d)),
            tile(D_INNER, 0),
            tile(D_INNER, C_Z // D_INNER),
            tile(3 * d, C_GATE // (3 * d)),
            _resident((1, D_INNER)),
            _resident((d, d)), _resident((d, d)), _resident((D_INNER, d)), _resident((d, d)),
        ],
        out_specs=tile(d, 0),
        compiler_params=_cparams(("parallel",)),
        name="merge",
    )(x, mod4, attn, proj, proj, proj, proj, proj, proj, proj, conv_w, y, proj, proj, norm_g, wa, wc, ws, wm)


FF_SPLIT = 11


def _ffn_kernel(x_ref, shift_ref, scale_ref, gate_ref, g_ref, w1_ref, w2_ref, o_ref):
    x = x_ref[...]
    h = _norm_mod(x, g_ref[...], scale_ref[...], shift_ref[...]).astype(BF16)
    slab = D_FF // FF_SPLIT
    acc = None
    for j in range(FF_SPLIT):
        hg = _dot(h, w1_ref[:, j * slab:(j + 1) * slab])
        hu = _dot(h, w1_ref[:, D_FF + j * slab:D_FF + (j + 1) * slab])
        part = _dot((_silu(hg) * hu).astype(BF16), w2_ref[j * slab:(j + 1) * slab, :])
        acc = part if acc is None else acc + part
    o_ref[...] = x + gate_ref[...] * acc


def _ffn(x, mod4, layer, seq, latent, norm_g, w1, w2):
    t = x.shape[0]
    tm = 1024
    tps = max(seq // tm, 1)
    row_fn = (lambda i: LAT_ROW0 + i // tps) if latent else (lambda i: CTX_ROW)
    return pl.pallas_call(
        _ffn_kernel,
        out_shape=jax.ShapeDtypeStruct((t, D_MODEL), F32),
        grid=(t // tm,),
        in_specs=[
            pl.BlockSpec((tm, D_MODEL), lambda i: (i, 0)),
            _mod_spec(layer, 3, row_fn),
            _mod_spec(layer, 4, row_fn),
            _mod_spec(layer, 5, row_fn),
            _resident((1, D_MODEL)),
            _resident((D_MODEL, 2 * D_FF)),
            _resident((D_FF, D_MODEL)),
        ],
        out_specs=pl.BlockSpec((tm, D_MODEL), lambda i: (i, 0)),
        compiler_params=_cparams(("parallel",)),
        name="ffn",
    )(x, mod4, mod4, mod4, norm_g, w1, w2)


def _rope_tables(n_tok):
    rows = n_tok // GRID_W
    row = jnp.repeat(jnp.arange(rows, dtype=F32), GRID_W)
    col = jnp.tile(jnp.arange(GRID_W, dtype=F32), rows)
    axis_dim = HEAD_DIM // 2
    inv_freq = 1.0 / (ROPE_THETA ** (jnp.arange(0, axis_dim, 2, dtype=F32) / axis_dim))
    ang = jnp.concatenate([row[:, None] * inv_freq, col[:, None] * inv_freq], axis=-1)
    cos = jnp.repeat(jnp.cos(ang), 2, axis=-1)
    sin = jnp.repeat(jnp.sin(ang), 2, axis=-1)
    sign = jnp.tile(jnp.asarray([-1.0, 1.0], F32), HEAD_DIM // 2)
    return cos, sin * sign


def _layer(x, seq, latent, layer, mod4, wts, cache_k4, cache_v4, s0, rope_tabs, sfin_all=None):
    proj, dt_raw, qk, vb, *v32 = _in_proj(x, mod4, layer, seq, latent, wts["norm1_g"], wts["w_main"], wts["w_dt"])
    attn, *k32 = _attention(qk, vb, seq, latent, layer, wts["q_g"], wts["k_g"], rope_tabs, cache_k4, cache_v4)
    kv = k32 + v32
    y, *s_fin = _ssd(proj, dt_raw, dt_raw.T, seq, wts["ssd_conv_w"], wts["ssd_conv_b"], wts["dt_bias"],
                     wts["a_log"], wts["d_wide"], layer, s0, not latent, sfin_all)
    x = _merge(x, mod4, layer, seq, latent, proj, attn, y, wts["conv_w"], wts["ssd_norm_g"],
               wts["w_attn_o"], wts["w_conv_o"], wts["w_ssd_o"], wts["w_merge"])
    x = _ffn(x, mod4, layer, seq, latent, wts["norm2_g"], wts["ffn_w1"], wts["ffn_w2"])
    return x, kv, s_fin


def _layer_weights(l, norm1_g, norm2_g, w_in, q_norm_g, k_norm_g, w_attn_o, conv_w, w_conv_o, ssd_conv_w,
                   ssd_conv_b, ssd_dt_bias, ssd_a_log, ssd_d, ssd_norm_g, w_ssd_o, w_merge, ffn_w1, ffn_w2):
    d = D_MODEL
    kvw = N_KV_HEADS * HEAD_DIM
    o_q, o_k, o_v, o_bg, o_cg, o_cx = 0, d, d + kvw, d + 2 * kvw, 2 * d + 2 * kvw, 3 * d + 2 * kvw
    o_z = o_cx + d
    o_xbc = o_z + D_INNER
    o_dt = o_xbc + XBC_W
    o_gate = o_dt + 2 * N_SSD_HEADS
    w = w_in[l]
    cols = lambda a, n: w[:, a:a + n]
    w_main = jnp.concatenate([cols(o_xbc, XBC_W), cols(o_z, D_INNER), cols(o_gate, 3 * d), cols(o_bg, d),
                              cols(o_cg, d), cols(o_cx, d), cols(o_q, d), cols(o_k, kvw), cols(o_v, kvw)],
                             axis=1).astype(BF16)
    w_dt = jnp.concatenate([cols(o_dt, 2 * N_SSD_HEADS), jnp.zeros((d, LANES - 2 * N_SSD_HEADS), F32)],
                           axis=1).astype(BF16)
    return {
        "norm1_g": norm1_g[l].reshape(1, d), "norm2_g": norm2_g[l].reshape(1, d),
        "w_main": w_main, "w_dt": w_dt,
        "q_g": q_norm_g[l].reshape(1, HEAD_DIM), "k_g": k_norm_g[l].reshape(1, HEAD_DIM),
        "w_attn_o": w_attn_o[l].astype(BF16), "conv_w": conv_w[l], "w_conv_o": w_conv_o[l].astype(BF16),
        "ssd_conv_w": ssd_conv_w[l], "ssd_conv_b": ssd_conv_b[l].reshape(1, -1),
        "dt_bias": ssd_dt_bias[l], "a_log": ssd_a_log[l],
        "d_wide": jnp.repeat(ssd_d[l], SSD_HEADDIM).reshape(1, D_INNER),
        "ssd_norm_g": ssd_norm_g[l].reshape(1, D_INNER), "w_ssd_o": w_ssd_o[l].astype(BF16),
        "w_merge": w_merge[l].astype(BF16), "ffn_w1": ffn_w1[l].astype(BF16), "ffn_w2": ffn_w2[l].astype(BF16),
    }


def kernel(x_prompt, x_sample, c, cache_k, cache_v, state_ssd, c_ctx, ada_w, ada_b, norm1_g, norm2_g, w_in,
           q_norm_g, k_norm_g, w_attn_o, conv_w, w_conv_o, ssd_conv_w, ssd_conv_b, ssd_dt_bias, ssd_a_log, ssd_d,
           ssd_norm_g, w_ssd_o, w_merge, ffn_w1, ffn_w2):
    nb_ctx, seq_ctx, d = x_prompt.shape
    nb_lat, seq_lat, _ = x_sample.shape
    past = cache_k.shape[2]
    cond = jnp.concatenate([c_ctx[None, :], c, jnp.zeros((MOD_ROWS - nb_lat - 1, d), F32)], axis=0)
    mod4 = _modulation(cond, ada_w, ada_b).reshape(DEPTH, MOD_ROWS, 1, 6 * d)
    cache_k4 = cache_k.reshape(nb_lat, DEPTH, past, N_KV_HEADS * HEAD_DIM)
    cache_v4 = cache_v.reshape(nb_lat, DEPTH, past, N_KV_HEADS * HEAD_DIM)
    rope_tabs = _rope_tables(seq_lat)
    s0_lat = state_ssd.reshape(nb_lat, DEPTH, 2, D_INNER, D_STATE)

    y_ctx = x_prompt.reshape(nb_ctx * seq_ctx, d)
    y_lat = x_sample.reshape(nb_lat * seq_lat, d)
    ks, vs = [], []
    sfin_all = None
    for l in range(DEPTH):
        wts = _layer_weights(l, norm1_g, norm2_g, w_in, q_norm_g, k_norm_g, w_attn_o, conv_w, w_conv_o,
                             ssd_conv_w, ssd_conv_b, ssd_dt_bias, ssd_a_log, ssd_d, ssd_norm_g, w_ssd_o,
                             w_merge, ffn_w1, ffn_w2)
        y_ctx, (k32, v32), (sfin_all,) = _layer(y_ctx, seq_ctx, False, l, mod4, wts, None, None, None, None,
                                                sfin_all)
        y_lat, _, _ = _layer(y_lat, seq_lat, True, l, mod4, wts, cache_k4, cache_v4, s0_lat, rope_tabs)
        ks.append(k32.reshape(nb_ctx, seq_ctx, N_KV_HEADS, HEAD_DIM))
        vs.append(v32.reshape(nb_ctx, seq_ctx, N_KV_HEADS, HEAD_DIM))
    new_ssd = sfin_all.reshape(nb_ctx, DEPTH, 2, N_SSD_HEADS, SSD_HEADDIM, D_STATE)
    return (y_ctx.reshape(nb_ctx, seq_ctx, d), y_lat.reshape(nb_lat, seq_lat, d),
            jnp.stack(ks, axis=1), jnp.stack(vs, axis=1), new_ssd)
```

```python
import functools

import jax
import jax.numpy as jnp
from jax import lax
from jax.experimental import pallas as pl
from jax.experimental.pallas import tpu as pltpu

F32 = jnp.float32
BF16 = jnp.bfloat16

D_MODEL = 1024
DEPTH = 2
GRID_W = 64
N_HEADS = 8
N_KV_HEADS = 2
HEAD_DIM = 128
KV_REP = N_HEADS // N_KV_HEADS
ROPE_THETA = 10000.0
CONV_K = 3
D_INNER = 2 * D_MODEL
SSD_HEADDIM = 64
N_SSD_HEADS = D_INNER // SSD_HEADDIM
N_SSD_GROUPS = 8
SSD_HPG = N_SSD_HEADS // N_SSD_GROUPS
D_STATE = 128
CHUNK = 128
D_FF = -(-8 * D_MODEL // 768) * 256
EPS = 1e-6

LANES = 128
BF16_ROWS = 16
VMEM_LIMIT = 56 * 1024 * 1024

XBC_W = 2 * D_INNER
C_XBC = 0
C_Z = C_XBC + XBC_W
C_GATE = C_Z + D_INNER
C_BG = C_GATE + 3 * D_MODEL
C_CG = C_BG + D_MODEL
C_CX = C_CG + D_MODEL
C_Q = C_CX + D_MODEL
C_K = C_Q + D_MODEL
C_V = C_K + N_KV_HEADS * HEAD_DIM
N_MAIN = C_V + N_KV_HEADS * HEAD_DIM
PROJ_TN = N_MAIN - C_Q
J_QKV = C_Q // PROJ_TN
LOG2E = 1.4426950408889634
MOD_ROWS = 16
CTX_ROW = 0
LAT_ROW0 = 1


def _cparams(sem):
    return pltpu.CompilerParams(dimension_semantics=sem, vmem_limit_bytes=VMEM_LIMIT)


def _dot(a, b):
    return jnp.dot(a, b, preferred_element_type=F32)


def _sigmoid(x):
    return 1.0 / (1.0 + jnp.exp(-x))


def _silu(x):
    return x * _sigmoid(x)


def _softplus(x):
    return jnp.maximum(x, 0.0) + jnp.log1p(jnp.exp(-jnp.abs(x)))


def _split3(x):
    hi = x.astype(BF16)
    r1 = x - hi.astype(F32)
    mid = r1.astype(BF16)
    lo = (r1 - mid.astype(F32)).astype(BF16)
    return hi, mid, lo


def _mod_kernel(c_ref, w_ref, b_ref, o_ref):
    s = _silu(c_ref[...]).astype(BF16)
    o_ref[...] = _dot(s, w_ref[...].astype(BF16)) + b_ref[...]


def _modulation(cond, ada_w, ada_b):
    tn = 1536
    return pl.pallas_call(
        _mod_kernel,
        out_shape=jax.ShapeDtypeStruct((DEPTH, MOD_ROWS, 6 * D_MODEL), F32),
        grid=(DEPTH, 6 * D_MODEL // tn),
        in_specs=[
            pl.BlockSpec((MOD_ROWS, D_MODEL), lambda l, j: (0, 0)),
            pl.BlockSpec((None, D_MODEL, tn), lambda l, j: (l, 0, j)),
            pl.BlockSpec((None, 1, tn), lambda l, j: (l, 0, j)),
        ],
        out_specs=pl.BlockSpec((None, MOD_ROWS, tn), lambda l, j: (l, 0, j)),
        compiler_params=_cparams(("parallel", "parallel")),
        name="modulation",
    )(cond, ada_w, ada_b.reshape(DEPTH, 1, 6 * D_MODEL))


def _mod_spec(layer, which, row_fn):
    return pl.BlockSpec((None, None, 1, D_MODEL), lambda *ids: (layer, row_fn(*ids), 0, which))


def _norm_mod(x, g, scale, shift):
    ms = jnp.mean(x * x, axis=-1, keepdims=True)
    return (x * lax.rsqrt(ms + EPS) * g) * (1.0 + scale) + shift


def _head_norm(a, g):
    ms = jnp.mean(a * a, axis=-1, keepdims=True)
    return a * lax.rsqrt(ms + EPS) * g


def _rope(a, cos, sin_signed):
    lane = lax.broadcasted_iota(jnp.int32, a.shape, 1)
    nxt = pltpu.roll(a, HEAD_DIM - 1, 1)
    prv = pltpu.roll(a, 1, 1)
    swapped = jnp.where(lane % 2 == 0, nxt, prv)
    return a * cos + swapped * sin_signed


def _inproj_kernel(latent, x_ref, shift_ref, scale_ref, g_ref, w_ref, wdt_ref, proj_ref, dt_ref, qk_ref, vb_ref,
                   *rest):
    if latent:
        (h_scr,) = rest
    else:
        v32_ref, h_scr = rest
    j = pl.program_id(1)

    @pl.when(j == 0)
    def _():
        h = _norm_mod(x_ref[...], g_ref[...], scale_ref[...], shift_ref[...]).astype(BF16)
        h_scr[...] = h
        dt_ref[...] = _dot(h, wdt_ref[...])

    def activation(col):
        if C_Z <= col < C_GATE:
            return _silu
        if C_GATE <= col < C_BG:
            return _sigmoid
        return None

    for jt in range(J_QKV):
        c0, c1 = jt * PROJ_TN, (jt + 1) * PROJ_TN
        cuts = [c0] + [c for c in (C_Z, C_GATE, C_BG) if c0 < c < c1] + [c1]

        @pl.when(j == jt)
        def _(c0=c0, cuts=cuts):
            for a, b in zip(cuts[:-1], cuts[1:]):
                acc = _dot(h_scr[...], w_ref[:, a - c0:b - c0])
                fn = activation(a)
                proj_ref[:, a - c0:b - c0] = (acc if fn is None else fn(acc)).astype(BF16)

    @pl.when(j == J_QKV)
    def _():
        acc = _dot(h_scr[...], w_ref[...])
        v0 = C_V - C_Q
        qk_ref[...] = acc[:, :v0]
        if not latent:
            v32_ref[...] = acc[:, v0:]
        vb_ref[...] = acc[:, v0:].astype(BF16)


QK_W = C_V - C_Q
KV_W = N_KV_HEADS * HEAD_DIM


def _in_proj(x, mod4, layer, seq, latent, norm_g, w_main, w_dt):
    t = x.shape[0]
    tm = 1024
    tiles_per_seq = max(seq // tm, 1)
    row_fn = (lambda i, j: LAT_ROW0 + i // tiles_per_seq) if latent else (lambda i, j: CTX_ROW)
    once = lambda width: pl.BlockSpec((tm, width), lambda i, j: (i, 0))
    in_specs = [
        pl.BlockSpec((tm, D_MODEL), lambda i, j: (i, 0)),
        _mod_spec(layer, 0, row_fn),
        _mod_spec(layer, 1, row_fn),
        pl.BlockSpec((1, D_MODEL), lambda i, j: (0, 0)),
        pl.BlockSpec((D_MODEL, PROJ_TN), lambda i, j: (0, j)),
        pl.BlockSpec((D_MODEL, LANES), lambda i, j: (0, 0)),
    ]
    out_shape = [jax.ShapeDtypeStruct((t, C_Q), BF16), jax.ShapeDtypeStruct((t, LANES), F32),
                 jax.ShapeDtypeStruct((t, QK_W), F32), jax.ShapeDtypeStruct((t, KV_W), BF16)]
    out_specs = [pl.BlockSpec((tm, PROJ_TN), lambda i, j: (i, jnp.minimum(j, J_QKV - 1))),
                 once(LANES), once(QK_W), once(KV_W)]
    if not latent:
        out_shape.append(jax.ShapeDtypeStruct((t, KV_W), F32))
        out_specs.append(once(KV_W))
    return pl.pallas_call(
        functools.partial(_inproj_kernel, latent),
        out_shape=out_shape,
        grid=(t // tm, N_MAIN // PROJ_TN),
        in_specs=in_specs,
        out_specs=out_specs,
        scratch_shapes=[pltpu.VMEM((tm, D_MODEL), BF16)],
        compiler_params=_cparams(("parallel", "arbitrary")),
        name="in_proj_lat" if latent else "in_proj_ctx",
    )(x, mod4, mod4, norm_g, w_main, w_dt)


def _attn_kernel(latent, q_ref, k_ref, v_ref, qg_ref, kg_ref, *rest):
    if latent:
        cosq_ref, sinq_ref, cosk_ref, sink_ref, ck_ref, cv_ref, o_ref, k_scr, v1_scr = rest
    else:
        has_prev = len(rest) == 7
        o_ref, kn_ref, vn_ref, k_scr, v1_scr = rest[-5:]
        k_slab = kn_ref if has_prev else kn_ref.at[0]
        v_slab = vn_ref if has_prev else vn_ref.at[0]
    seq = k_ref.shape[0]
    nt = (((1,), (1,)), ((), ()))

    @pl.when(pl.program_id(2) == 0)
    def _():
        k = _head_norm(k_ref[...], kg_ref[...])
        v = v_ref[...]
        if latent:
            k = _rope(k, cosk_ref[...], sink_ref[...])
            k_scr[seq:, :] = ck_ref[...].astype(BF16)
            v1_scr[seq:, 0:HEAD_DIM] = cv_ref[...].astype(BF16)
        else:
            g = pl.program_id(1)
            if not has_prev:
                @pl.when(g == 0)
                def _():
                    kn_ref[1:] = jnp.zeros((DEPTH - 1,) + tuple(kn_ref.shape[1:]), F32)
                    vn_ref[1:] = jnp.zeros((DEPTH - 1,) + tuple(vn_ref.shape[1:]), F32)
            for gg in range(N_KV_HEADS):
                @pl.when(g == gg)
                def _(gg=gg):
                    k_slab[:, gg, :] = k
                    v_slab[:, gg, :] = v
        k_scr[0:seq, :] = k.astype(BF16)
        v1_scr[0:seq, 0:HEAD_DIM] = v.astype(BF16)
        v1_scr[:, HEAD_DIM:] = jnp.ones((v1_scr.shape[0], HEAD_DIM), BF16)

    q_scale = HEAD_DIM ** -0.5 * LOG2E
    for hh in range(KV_REP):
        sl = slice(hh * HEAD_DIM, (hh + 1) * HEAD_DIM)
        q = _head_norm(q_ref[:, sl], qg_ref[...])
        if latent:
            q = _rope(q, cosq_ref[...], sinq_ref[...])
        s = lax.dot_general((q * q_scale).astype(BF16), k_scr[...], nt, preferred_element_type=F32)
        m = jnp.max(s, axis=-1, keepdims=True)
        o = _dot(jnp.exp2(s - m).astype(BF16), v1_scr[...])
        o_ref[:, sl] = (o[:, :HEAD_DIM] / o[:, HEAD_DIM:]).astype(BF16)


def _attention(qk, v, seq, latent, layer, q_g, k_g, rope_tabs, cache_k4, cache_v4, kv_all=None):
    t = qk.shape[0]
    nb = t // seq
    tq = min(512, seq)
    qt = seq // tq
    gw = KV_REP * HEAD_DIM
    const = lambda b, g, i: (0, 0)
    in_specs = [
        pl.BlockSpec((tq, gw), lambda b, g, i: (b * qt + i, g)),
        pl.BlockSpec((seq, HEAD_DIM), lambda b, g, i: (b, N_HEADS + g)),
        pl.BlockSpec((seq, HEAD_DIM), lambda b, g, i: (b, g)),
        pl.BlockSpec((1, HEAD_DIM), const), pl.BlockSpec((1, HEAD_DIM), const),
    ]
    args = [qk, qk, v, q_g, k_g]
    past = cache_k4.shape[2] if latent else 0
    out_shape = [jax.ShapeDtypeStruct((t, N_HEADS * HEAD_DIM), BF16)]
    out_specs = [pl.BlockSpec((tq, gw), lambda b, g, i: (b * qt + i, g))]
    aliases = {}
    if latent:
        tab_q = pl.BlockSpec((tq, HEAD_DIM), lambda b, g, i: (i, 0))
        tab_k = pl.BlockSpec((seq, HEAD_DIM), const)
        cspec = pl.BlockSpec((None, None, past, HEAD_DIM), lambda b, g, i: (b, layer, 0, g))
        in_specs += [tab_q, tab_q, tab_k, tab_k, cspec, cspec]
        args += [rope_tabs[0], rope_tabs[1], rope_tabs[0], rope_tabs[1], cache_k4, cache_v4]
    else:
        kv_shape = (nb, DEPTH, seq, N_KV_HEADS, HEAD_DIM)
        out_shape += [jax.ShapeDtypeStruct(kv_shape, F32)] * 2
        if kv_all is not None:
            aliases = {len(args): 1, len(args) + 1: 2}
            in_specs += [pl.BlockSpec(memory_space=pl.ANY)] * 2
            args += list(kv_all)
            slab = pl.BlockSpec((None, None, seq, N_KV_HEADS, HEAD_DIM), lambda b, g, i: (b, layer, 0, 0, 0))
        else:
            assert layer == 0
            slab = pl.BlockSpec((None, DEPTH, seq, N_KV_HEADS, HEAD_DIM), lambda b, g, i: (b, 0, 0, 0, 0))
        out_specs += [slab, slab]
    return pl.pallas_call(
        functools.partial(_attn_kernel, latent),
        out_shape=out_shape,
        grid=(nb, N_KV_HEADS, qt),
        in_specs=in_specs,
        out_specs=out_specs,
        scratch_shapes=[pltpu.VMEM((seq + past, HEAD_DIM), BF16), pltpu.VMEM((seq + past, 2 * HEAD_DIM), BF16)],
        input_output_aliases=aliases,
        compiler_params=_cparams(("parallel", "parallel" if latent else "arbitrary", "arbitrary")),
        name="attention_lat" if latent else "attention_ctx",
    )(*args)


def _conv3(p, prev_row, next_row, w, first, last):
    tm = p.shape[0]
    sub = 8
    rid = lax.broadcasted_iota(jnp.int32, (sub, p.shape[1]), 0)
    prev_row = jnp.where(first, 0.0, prev_row)
    next_row = jnp.where(last, 0.0, next_row)
    dn = pltpu.roll(p, 1, 0)
    up = pltpu.roll(p, tm - 1, 0)
    dn = jnp.concatenate([jnp.where(rid == 0, prev_row, dn[:sub, :]), dn[sub:, :]], axis=0)
    up = jnp.concatenate([up[:tm - sub, :], jnp.where(rid == sub - 1, next_row, up[tm - sub:, :])], axis=0)
    return w[0:1, :] * dn + w[1:2, :] * p + w[2:3, :] * up


GW = SSD_HPG * SSD_HEADDIM
NBC = N_SSD_GROUPS * D_STATE
CONV_SLAB = 1024


def _ssd_chunk(rev, xcols, dtr, dtr_t, brow_ref, bcol_ref, arow_ref, acol_ref, st_scr, d_ref, emit):
    ii = lax.broadcasted_iota(jnp.int32, (CHUNK, CHUNK), 0)
    jj = lax.broadcasted_iota(jnp.int32, (CHUNK, CHUNK), 1)
    keep = (ii <= jj) if rev else (ii >= jj)
    tri = jnp.where(keep, 1.0, 0.0).astype(BF16)
    tri_t = jnp.where((jj <= ii) if rev else (jj >= ii), 1.0, 0.0).astype(BF16)

    dt = _softplus(dtr + brow_ref[...])
    acs = sum(_dot(tri, part) for part in _split3(dt * (-jnp.exp(arow_ref[...]))))
    dt_t = _softplus(dtr_t + bcol_ref[...])
    acs_t = sum(_dot(part, tri_t) for part in _split3(dt_t * (-jnp.exp(acol_ref[...]))))

    edge = 0 if rev else CHUNK - 1
    ea = jnp.exp(acs)
    dec_row = ea[edge:edge + 1, :]
    w_t = dt_t * jnp.exp(acs_t[:, edge:edge + 1] - acs_t)
    acs2 = acs * LOG2E
    l2_t = acs_t * LOG2E - jnp.log2(dt_t)
    head_of_lane = lax.broadcasted_iota(jnp.int32, (CHUNK, GW), 1) // SSD_HEADDIM
    head_of_lane_row = lax.broadcasted_iota(jnp.int32, (1, GW), 1) // SSD_HEADDIM
    lane0 = N_SSD_HEADS if rev else 0
    zero_blk = jnp.zeros((D_STATE, CHUNK), BF16)

    for g in range(N_SSD_GROUPS):
        gs = slice(g * GW, (g + 1) * GW)
        xs = xcols(g * GW, (g + 1) * GW)
        bc = xcols(D_INNER + g * D_STATE, D_INNER + (g + 1) * D_STATE)
        cc = xcols(D_INNER + NBC + g * D_STATE, D_INNER + NBC + (g + 1) * D_STATE)
        bc_t = bc.astype(F32).T
        cc_f = cc.astype(F32)
        cb = _dot(cc, bc_t.astype(BF16))
        st = st_scr[:, gs]
        st_b = st.astype(BF16)
        acc = None
        dec_w = jnp.zeros((1, GW), F32)
        for r in range(SSD_HPG):
            ln = lane0 + g * SSD_HPG + r
            seg = acs2[:, ln:ln + 1] - l2_t[ln:ln + 1, :]
            m = cb * jnp.exp2(jnp.where(keep, seg, -jnp.inf))
            cce = cc_f * ea[:, ln:ln + 1]
            bw = bc_t * w_t[ln:ln + 1, :]
            lhs = jnp.concatenate([jnp.concatenate([m.astype(BF16), cce.astype(BF16)], axis=1),
                                   jnp.concatenate([bw.astype(BF16), zero_blk], axis=1)], axis=0)
            sel = head_of_lane == r
            rhs = jnp.concatenate([jnp.where(sel, xs, jnp.zeros_like(xs)),
                                   jnp.where(sel, st_b, jnp.zeros_like(st_b))], axis=0)
            part = _dot(lhs, rhs)
            acc = part if acc is None else acc + part
            dec_w = jnp.where(head_of_lane_row == r, dec_row[:, ln:ln + 1], dec_w)
        y = acc[:CHUNK, :]
        if not rev:
            y = y + d_ref[:, gs] * xs.astype(F32)
        emit(g, y)
        st_scr[:, gs] = st * dec_w + acc[CHUNK:, :]


SSD_MAX_CPS = 4


def _ssd_kernel(nsteps, cps, has_init, has_prev, emit_final, xbc_ref, xp_ref, xn_ref, cw_ref, cb_ref, dtr_ref,
                dtrt_ref, brow_ref, bcol_ref, arow_ref, acol_ref, d_ref, *rest):
    rest = list(rest)
    s0_ref = rest.pop(0) if has_init else None
    if has_prev:
        rest.pop(0)
    y_ref = rest.pop(0)
    sfin_ref = rest.pop(0) if emit_final else None
    xact_scr, yf_scr, st_scr = rest
    s = pl.program_id(1)
    scan_args = (brow_ref, bcol_ref, arow_ref, acol_ref, st_scr, d_ref)

    def scan(rev, blk, k, emit):
        rows = slice(k * CHUNK, (k + 1) * CHUNK)
        _ssd_chunk(rev, lambda lo, hi: xact_scr[blk, rows, lo:hi], dtr_ref[rows, :], dtrt_ref[:, rows],
                   *scan_args, functools.partial(emit, rows))

    def load_state(d):
        if not has_init:
            st_scr[...] = jnp.zeros_like(st_scr)
            return
        for k in range(D_INNER // LANES):
            sl = slice(k * LANES, (k + 1) * LANES)
            st_scr[:, sl] = s0_ref[d, sl, :].T

    slab_ref = None
    if emit_final:
        slab_ref = sfin_ref if has_prev else sfin_ref.at[0]

    def store_state(d):
        if not emit_final:
            return
        for k in range(D_INNER // LANES):
            sl = slice(k * LANES, (k + 1) * LANES)
            slab_ref[d, sl, :] = st_scr[:, sl].T

    @pl.when(s == 0)
    def _():
        load_state(0)
        if emit_final and not has_prev:
            sfin_ref[1:] = jnp.zeros((DEPTH - 1,) + tuple(sfin_ref.shape[1:]), F32)

    @pl.when(s < nsteps)
    def _():
        blk = s
        r = BF16_ROWS - 1
        for k in range(XBC_W // CONV_SLAB):
            sl = slice(k * CONV_SLAB, (k + 1) * CONV_SLAB)
            conv = _conv3(xbc_ref[:, sl].astype(F32), xp_ref[r:r + 1, sl].astype(F32),
                          xn_ref[0:1, sl].astype(F32), cw_ref[:, sl], s == 0, s == nsteps - 1)
            xact_scr[blk, :, sl] = _silu(conv + cb_ref[:, sl]).astype(BF16)

        def emit(rows, g, y):
            yf_scr[blk, rows, g * GW:(g + 1) * GW] = y.astype(BF16)

        for k in range(cps):
            scan(False, blk, k, emit)

    @pl.when(s == nsteps - 1)
    def _():
        store_state(0)

    @pl.when(s == nsteps)
    def _():
        load_state(1)

    @pl.when(s >= nsteps)
    def _():
        blk = 2 * nsteps - 1 - s

        def emit(rows, g, y):
            gs = slice(g * GW, (g + 1) * GW)
            y_ref[rows, gs] = (y + yf_scr[blk, rows, gs].astype(F32)).astype(BF16)

        for k in reversed(range(cps)):
            scan(True, blk, k, emit)

    @pl.when(s == 2 * nsteps - 1)
    def _():
        store_state(1)


def _ssd(proj, dt_raw, dt_raw_t, seq, conv_w, conv_b, bias, a_log, d_wide, layer, s0, emit_final, sfin_all):
    t = proj.shape[0]
    nb = t // seq
    cps = min(SSD_MAX_CPS, seq // CHUNK)
    rows = cps * CHUNK
    nchunks = nsteps = seq // rows
    rpc = rows // BF16_ROWS
    last_blk = t // BF16_ROWS - 1
    xb = C_XBC // XBC_W

    def chunk_in(b, s):
        return b * nchunks + jnp.minimum(s, nchunks - 1)

    def chunk_of(b, s):
        return b * nchunks + jnp.where(s < nchunks, s, 2 * nchunks - 1 - s)

    def chunk_out(b, s):
        return b * nchunks + jnp.where(s < nchunks, nchunks - 1, 2 * nchunks - 1 - s)

    pad = jnp.zeros((LANES - 2 * N_SSD_HEADS,), F32)
    b_row = jnp.concatenate([bias.reshape(-1), pad]).reshape(1, LANES)
    a_row = jnp.concatenate([a_log.reshape(-1), pad]).reshape(1, LANES)
    const = lambda b, s: (0, 0)
    in_specs = [
        pl.BlockSpec((rows, XBC_W), lambda b, s: (chunk_in(b, s), xb)),
        pl.BlockSpec((BF16_ROWS, XBC_W), lambda b, s: (jnp.maximum(chunk_in(b, s) * rpc - 1, 0), xb)),
        pl.BlockSpec((BF16_ROWS, XBC_W), lambda b, s: (jnp.minimum((chunk_in(b, s) + 1) * rpc, last_blk), xb)),
        pl.BlockSpec((CONV_K, XBC_W), const),
        pl.BlockSpec((1, XBC_W), const),
        pl.BlockSpec((rows, LANES), lambda b, s: (chunk_of(b, s), 0)),
        pl.BlockSpec((LANES, rows), lambda b, s: (0, chunk_of(b, s))),
        pl.BlockSpec((1, LANES), const), pl.BlockSpec((LANES, 1), const),
        pl.BlockSpec((1, LANES), const), pl.BlockSpec((LANES, 1), const),
        pl.BlockSpec((1, D_INNER), const),
    ]
    args = [proj, proj, proj, conv_w, conv_b, dt_raw, dt_raw_t, b_row, b_row.reshape(LANES, 1), a_row,
            a_row.reshape(LANES, 1), d_wide]
    out_shape = [jax.ShapeDtypeStruct((t, D_INNER), BF16)]
    out_specs = [pl.BlockSpec((rows, D_INNER), lambda b, s: (chunk_out(b, s), 0))]
    state_spec = pl.BlockSpec((None, None, 2, D_INNER, D_STATE), lambda b, s: (b, layer, 0, 0, 0))
    aliases = {}
    if s0 is not None:
        in_specs.append(state_spec)
        args.append(s0)
    if emit_final:
        out_shape.append(jax.ShapeDtypeStruct((nb, DEPTH, 2, D_INNER, D_STATE), F32))
        if sfin_all is not None:
            aliases = {len(args): 1}
            in_specs.append(pl.BlockSpec(memory_space=pl.ANY))
            args.append(sfin_all)
            out_specs.append(state_spec)
        else:
            assert layer == 0
            out_specs.append(pl.BlockSpec((None, DEPTH, 2, D_INNER, D_STATE), lambda b, s: (b, 0, 0, 0, 0)))
    return pl.pallas_call(
        functools.partial(_ssd_kernel, nsteps, cps, s0 is not None, bool(aliases), emit_final),
        out_shape=out_shape,
        grid=(nb, 2 * nsteps),
        in_specs=in_specs,
        out_specs=out_specs,
        input_output_aliases=aliases,
        scratch_shapes=[pltpu.VMEM((nsteps, rows, XBC_W), BF16), pltpu.VMEM((nsteps, rows, D_INNER), BF16),
                        pltpu.VMEM((D_STATE, D_INNER), F32)],
        compiler_params=_cparams(("parallel", "arbitrary")),
        name="ssd_lat" if s0 is not None else "ssd_ctx",
    )(*args)


def _merge_kernel(tm, seq, x_ref, gate_ref, attn_ref, bg_ref, cg_ref, cx_ref, cgp_ref, cxp_ref, cgn_ref, cxn_ref,
                  cw_ref, y_ref, z_ref, gl_ref, ng_ref, wa_ref, wc_ref, ws_ref, wm_ref, o_ref):
    i = pl.program_id(0)
    tps = seq // tm
    r = BF16_ROWS - 1
    p = cg_ref[...].astype(F32) * cx_ref[...].astype(F32)
    prev_row = cgp_ref[r:r + 1, :].astype(F32) * cxp_ref[r:r + 1, :].astype(F32)
    next_row = cgn_ref[0:1, :].astype(F32) * cxn_ref[0:1, :].astype(F32)
    conv = _conv3(p, prev_row, next_row, cw_ref[...], (i % tps) == 0, (i % tps) == tps - 1)
    u = (bg_ref[...].astype(F32) * conv).astype(BF16)
    tg = y_ref[...].astype(F32) * z_ref[...].astype(F32)
    ms = jnp.mean(tg * tg, axis=-1, keepdims=True)
    yn = (tg * lax.rsqrt(ms + EPS) * ng_ref[...]).astype(BF16)
    y_ssd = _dot(yn, ws_ref[...])
    y_attn = _dot(attn_ref[...], wa_ref[...])
    y_conv = _dot(u, wc_ref[...])
    d = D_MODEL
    mix = (gl_ref[:, 0:d].astype(F32) * y_attn + gl_ref[:, d:2 * d].astype(F32) * y_conv
           + gl_ref[:, 2 * d:3 * d].astype(F32) * y_ssd)
    o_ref[...] = x_ref[...] + gate_ref[...] * _dot(mix.astype(BF16), wm_ref[...])


def _resident(shape):
    return pl.BlockSpec(shape, lambda *_: (0,) * len(shape), pipeline_mode=pl.Buffered(1))


def _merge(x, mod4, layer, seq, latent, proj, attn, y, conv_w, norm_g, wa, wc, ws, wm):
    t = x.shape[0]
    tm = min(512, seq)
    tps = seq // tm
    row_fn = (lambda i: LAT_ROW0 + i // tps) if latent else (lambda i: CTX_ROW)
    d = D_MODEL
    rpt = tm // BF16_ROWS
    last_blk = t // BF16_ROWS - 1

    def tile(width, col_block):
        return pl.BlockSpec((tm, width), lambda i: (i, col_block))

    def prev(cb):
        return pl.BlockSpec((BF16_ROWS, d), lambda i: (jnp.maximum(i * rpt - 1, 0), cb))

    def nxt(cb):
        return pl.BlockSpec((BF16_ROWS, d), lambda i: (jnp.minimum((i + 1) * rpt, last_blk), cb))

    return pl.pallas_call(
        functools.partial(_merge_kernel, tm, seq),
        out_shape=jax.ShapeDtypeStruct((t, d), F32),
        grid=(t // tm,),
        in_specs=[
            tile(d, 0),
            _mod_spec(layer, 2, row_fn),
            tile(d, 0),
            tile(d, C_BG // d), tile(d, C_CG // d), tile(d, C_CX // d),
            prev(C_CG // d), prev(C_CX // d), nxt(C_CG // d), nxt(C_CX // d),
            _resident((CONV_K, d)),
            tile(D_INNER, 0),
            tile(D_INNER, C_Z // D_INNER),
            tile(3 * d, C_GATE // (3 * d)),
            _resident((1, D_INNER)),
            _resident((d, d)), _resident((d, d)), _resident((D_INNER, d)), _resident((d, d)),
        ],
        out_specs=tile(d, 0),
        compiler_params=_cparams(("parallel",)),
        name="merge",
    )(x, mod4, attn, proj, proj, proj, proj, proj, proj, proj, conv_w, y, proj, proj, norm_g, wa, wc, ws, wm)


FF_SPLIT = 11


def _ffn_kernel(x_ref, shift_ref, scale_ref, gate_ref, g_ref, w1_ref, w2_ref, o_ref):
    x = x_ref[...]
    h = _norm_mod(x, g_ref[...], scale_ref[...], shift_ref[...]).astype(BF16)
    slab = D_FF // FF_SPLIT
    acc = None
    for j in range(FF_SPLIT):
        hg = _dot(h, w1_ref[:, j * slab:(j + 1) * slab])
        hu = _dot(h, w1_ref[:, D_FF + j * slab:D_FF + (j + 1) * slab])
        part = _dot((_silu(hg) * hu).astype(BF16), w2_ref[j * slab:(j + 1) * slab, :])
        acc = part if acc is None else acc + part
    o_ref[...] = x + gate_ref[...] * acc


def _ffn(x, mod4, layer, seq, latent, norm_g, w1, w2):
    t = x.shape[0]
    tm = 1024
    tps = max(seq // tm, 1)
    row_fn = (lambda i: LAT_ROW0 + i // tps) if latent else (lambda i: CTX_ROW)
    return pl.pallas_call(
        _ffn_kernel,
        out_shape=jax.ShapeDtypeStruct((t, D_MODEL), F32),
        grid=(t // tm,),
        in_specs=[
            pl.BlockSpec((tm, D_MODEL), lambda i: (i, 0)),
            _mod_spec(layer, 3, row_fn),
            _mod_spec(layer, 4, row_fn),
            _mod_spec(layer, 5, row_fn),
            _resident((1, D_MODEL)),
            _resident((D_MODEL, 2 * D_FF)),
            _resident((D_FF, D_MODEL)),
        ],
        out_specs=pl.BlockSpec((tm, D_MODEL), lambda i: (i, 0)),
        compiler_params=_cparams(("parallel",)),
        name="ffn",
    )(x, mod4, mod4, mod4, norm_g, w1, w2)


def _rope_tables(n_tok):
    rows = n_tok // GRID_W
    row = jnp.repeat(jnp.arange(rows, dtype=F32), GRID_W)
    col = jnp.tile(jnp.arange(GRID_W, dtype=F32), rows)
    axis_dim = HEAD_DIM // 2
    inv_freq = 1.0 / (ROPE_THETA ** (jnp.arange(0, axis_dim, 2, dtype=F32) / axis_dim))
    ang = jnp.concatenate([row[:, None] * inv_freq, col[:, None] * inv_freq], axis=-1)
    cos = jnp.repeat(jnp.cos(ang), 2, axis=-1)
    sin = jnp.repeat(jnp.sin(ang), 2, axis=-1)
    sign = jnp.tile(jnp.asarray([-1.0, 1.0], F32), HEAD_DIM // 2)
    return cos, sin * sign


def _layer(x, seq, latent, layer, mod4, wts, cache_k4, cache_v4, s0, rope_tabs, kv_all=None, sfin_all=None):
    proj, dt_raw, qk, vb, *v32 = _in_proj(x, mod4, layer, seq, latent, wts["norm1_g"], wts["w_main"], wts["w_dt"])
    attn, *kv = _attention(qk, vb if latent else v32[0], seq, latent, layer, wts["q_g"], wts["k_g"], rope_tabs,
                           cache_k4, cache_v4, kv_all)
    y, *s_fin = _ssd(proj, dt_raw, dt_raw.T, seq, wts["ssd_conv_w"], wts["ssd_conv_b"], wts["dt_bias"],
                     wts["a_log"], wts["d_wide"], layer, s0, not latent, sfin_all)
    x = _merge(x, mod4, layer, seq, latent, proj, attn, y, wts["conv_w"], wts["ssd_norm_g"],
               wts["w_attn_o"], wts["w_conv_o"], wts["w_ssd_o"], wts["w_merge"])
    x = _ffn(x, mod4, layer, seq, latent, wts["norm2_g"], wts["ffn_w1"], wts["ffn_w2"])
    return x, kv, s_fin


def _layer_weights(l, norm1_g, norm2_g, w_in, q_norm_g, k_norm_g, w_attn_o, conv_w, w_conv_o, ssd_conv_w,
                   ssd_conv_b, ssd_dt_bias, ssd_a_log, ssd_d, ssd_norm_g, w_ssd_o, w_merge, ffn_w1, ffn_w2):
    d = D_MODEL
    kvw = N_KV_HEADS * HEAD_DIM
    o_q, o_k, o_v, o_bg, o_cg, o_cx = 0, d, d + kvw, d + 2 * kvw, 2 * d + 2 * kvw, 3 * d + 2 * kvw
    o_z = o_cx + d
    o_xbc = o_z + D_INNER
    o_dt = o_xbc + XBC_W
    o_gate = o_dt + 2 * N_SSD_HEADS
    w = w_in[l]
    cols = lambda a, n: w[:, a:a + n]
    w_main = jnp.concatenate([cols(o_xbc, XBC_W), cols(o_z, D_INNER), cols(o_gate, 3 * d), cols(o_bg, d),
                              cols(o_cg, d), cols(o_cx, d), cols(o_q, d), cols(o_k, kvw), cols(o_v, kvw)],
                             axis=1).astype(BF16)
    w_dt = jnp.concatenate([cols(o_dt, 2 * N_SSD_HEADS), jnp.zeros((d, LANES - 2 * N_SSD_HEADS), F32)],
                           axis=1).astype(BF16)
    return {
        "norm1_g": norm1_g[l].reshape(1, d), "norm2_g": norm2_g[l].reshape(1, d),
        "w_main": w_main, "w_dt": w_dt,
        "q_g": q_norm_g[l].reshape(1, HEAD_DIM), "k_g": k_norm_g[l].reshape(1, HEAD_DIM),
        "w_attn_o": w_attn_o[l].astype(BF16), "conv_w": conv_w[l], "w_conv_o": w_conv_o[l].astype(BF16),
        "ssd_conv_w": ssd_conv_w[l], "ssd_conv_b": ssd_conv_b[l].reshape(1, -1),
        "dt_bias": ssd_dt_bias[l], "a_log": ssd_a_log[l],
        "d_wide": jnp.repeat(ssd_d[l], SSD_HEADDIM).reshape(1, D_INNER),
        "ssd_norm_g": ssd_norm_g[l].reshape(1, D_INNER), "w_ssd_o": w_ssd_o[l].astype(BF16),
        "w_merge": w_merge[l].astype(BF16), "ffn_w1": ffn_w1[l].astype(BF16), "ffn_w2": ffn_w2[l].astype(BF16),
    }


def kernel(x_prompt, x_sample, c, cache_k, cache_v, state_ssd, c_ctx, ada_w, ada_b, norm1_g, norm2_g, w_in,
           q_norm_g, k_norm_g, w_attn_o, conv_w, w_conv_o, ssd_conv_w, ssd_conv_b, ssd_dt_bias, ssd_a_log, ssd_d,
           ssd_norm_g, w_ssd_o, w_merge, ffn_w1, ffn_w2):
    nb_ctx, seq_ctx, d = x_prompt.shape
    nb_lat, seq_lat, _ = x_sample.shape
    past = cache_k.shape[2]
    cond = jnp.concatenate([c_ctx[None, :], c, jnp.zeros((MOD_ROWS - nb_lat - 1, d), F32)], axis=0)
    mod4 = _modulation(cond, ada_w, ada_b).reshape(DEPTH, MOD_ROWS, 1, 6 * d)
    cache_k4 = cache_k.reshape(nb_lat, DEPTH, past, N_KV_HEADS * HEAD_DIM)
    cache_v4 = cache_v.reshape(nb_lat, DEPTH, past, N_KV_HEADS * HEAD_DIM)
    rope_tabs = _rope_tables(seq_lat)
    s0_lat = state_ssd.reshape(nb_lat, DEPTH, 2, D_INNER, D_STATE)

    y_ctx = x_prompt.reshape(nb_ctx * seq_ctx, d)
    y_lat = x_sample.reshape(nb_lat * seq_lat, d)
    kv_all = sfin_all = None
    for l in range(DEPTH):
        wts = _layer_weights(l, norm1_g, norm2_g, w_in, q_norm_g, k_norm_g, w_attn_o, conv_w, w_conv_o,
                             ssd_conv_w, ssd_conv_b, ssd_dt_bias, ssd_a_log, ssd_d, ssd_norm_g, w_ssd_o,
                             w_merge, ffn_w1, ffn_w2)
        y_ctx, kv_all, (sfin_all,) = _layer(y_ctx, seq_ctx, False, l, mod4, wts, None, None, None, None,
                                            kv_all, sfin_all)
        y_lat, _, _ = _layer(y_lat, seq_lat, True, l, mod4, wts, cache_k4, cache_v4, s0_lat, rope_tabs)
    new_k, new_v = kv_all
    new_ssd = sfin_all.reshape(nb_ctx, DEPTH, 2, N_SSD_HEADS, SSD_HEADDIM, D_STATE)
    return (y_ctx.reshape(nb_ctx, seq_ctx, d), y_lat.reshape(nb_lat, seq_lat, d), new_k, new_v, new_ssd)
```

```python
import functools

import jax
import jax.numpy as jnp
from jax import lax
from jax.experimental import pallas as pl
from jax.experimental.pallas import tpu as pltpu

F32 = jnp.float32
BF16 = jnp.bfloat16

D_MODEL = 1024
DEPTH = 2
GRID_W = 64
N_HEADS = 8
N_KV_HEADS = 2
HEAD_DIM = 128
KV_REP = N_HEADS // N_KV_HEADS
ROPE_THETA = 10000.0
CONV_K = 3
D_INNER = 2 * D_MODEL
SSD_HEADDIM = 64
N_SSD_HEADS = D_INNER // SSD_HEADDIM
N_SSD_GROUPS = 8
SSD_HPG = N_SSD_HEADS // N_SSD_GROUPS
D_STATE = 128
CHUNK = 128
D_FF = -(-8 * D_MODEL // 768) * 256
EPS = 1e-6

LANES = 128
BF16_ROWS = 16
VMEM_LIMIT = 56 * 1024 * 1024

XBC_W = 2 * D_INNER
C_XBC = 0
C_Z = C_XBC + XBC_W
C_GATE = C_Z + D_INNER
C_BG = C_GATE + 3 * D_MODEL
C_CG = C_BG + D_MODEL
C_CX = C_CG + D_MODEL
C_Q = C_CX + D_MODEL
C_K = C_Q + D_MODEL
C_V = C_K + N_KV_HEADS * HEAD_DIM
N_MAIN = C_V + N_KV_HEADS * HEAD_DIM
PROJ_TN = N_MAIN - C_Q
J_QKV = C_Q // PROJ_TN
LOG2E = 1.4426950408889634
MOD_ROWS = 16
CTX_ROW = 0
LAT_ROW0 = 1


def _cparams(sem):
    return pltpu.CompilerParams(dimension_semantics=sem, vmem_limit_bytes=VMEM_LIMIT)


def _dot(a, b):
    return jnp.dot(a, b, preferred_element_type=F32)


def _sigmoid(x):
    return 1.0 / (1.0 + jnp.exp(-x))


def _silu(x):
    return x * _sigmoid(x)


def _softplus(x):
    return jnp.maximum(x, 0.0) + jnp.log1p(jnp.exp(-jnp.abs(x)))


def _split3(x):
    hi = x.astype(BF16)
    r1 = x - hi.astype(F32)
    mid = r1.astype(BF16)
    lo = (r1 - mid.astype(F32)).astype(BF16)
    return hi, mid, lo


def _mod_kernel(c_ref, w_ref, b_ref, o_ref):
    s = _silu(c_ref[...]).astype(BF16)
    o_ref[...] = _dot(s, w_ref[...].astype(BF16)) + b_ref[...]


def _modulation(cond, ada_w, ada_b):
    tn = 1536
    return pl.pallas_call(
        _mod_kernel,
        out_shape=jax.ShapeDtypeStruct((DEPTH, MOD_ROWS, 6 * D_MODEL), F32),
        grid=(DEPTH, 6 * D_MODEL // tn),
        in_specs=[
            pl.BlockSpec((MOD_ROWS, D_MODEL), lambda l, j: (0, 0)),
            pl.BlockSpec((None, D_MODEL, tn), lambda l, j: (l, 0, j)),
            pl.BlockSpec((None, 1, tn), lambda l, j: (l, 0, j)),
        ],
        out_specs=pl.BlockSpec((None, MOD_ROWS, tn), lambda l, j: (l, 0, j)),
        compiler_params=_cparams(("parallel", "parallel")),
        name="modulation",
    )(cond, ada_w, ada_b.reshape(DEPTH, 1, 6 * D_MODEL))


def _mod_spec(layer, which, row_fn):
    return pl.BlockSpec((None, None, 1, D_MODEL), lambda *ids: (layer, row_fn(*ids), 0, which))


def _norm_mod(x, g, scale, shift):
    ms = jnp.mean(x * x, axis=-1, keepdims=True)
    return (x * lax.rsqrt(ms + EPS) * g) * (1.0 + scale) + shift


def _head_norm(a, g):
    ms = jnp.mean(a * a, axis=-1, keepdims=True)
    return a * lax.rsqrt(ms + EPS) * g


def _rope(a, cos, sin_signed):
    lane = lax.broadcasted_iota(jnp.int32, a.shape, 1)
    nxt = pltpu.roll(a, HEAD_DIM - 1, 1)
    prv = pltpu.roll(a, 1, 1)
    swapped = jnp.where(lane % 2 == 0, nxt, prv)
    return a * cos + swapped * sin_signed


def _inproj_kernel(latent, x_ref, shift_ref, scale_ref, g_ref, w_ref, wdt_ref, proj_ref, dt_ref, qk_ref, vb_ref,
                   *rest):
    if latent:
        (h_scr,) = rest
    else:
        v32_ref, h_scr = rest
    j = pl.program_id(1)

    @pl.when(j == 0)
    def _():
        h = _norm_mod(x_ref[...], g_ref[...], scale_ref[...], shift_ref[...]).astype(BF16)
        h_scr[...] = h
        dt_ref[...] = _dot(h, wdt_ref[...])

    def activation(col):
        if C_Z <= col < C_GATE:
            return _silu
        if C_GATE <= col < C_BG:
            return _sigmoid
        return None

    for jt in range(J_QKV):
        c0, c1 = jt * PROJ_TN, (jt + 1) * PROJ_TN
        cuts = [c0] + [c for c in (C_Z, C_GATE, C_BG) if c0 < c < c1] + [c1]

        @pl.when(j == jt)
        def _(c0=c0, cuts=cuts):
            for a, b in zip(cuts[:-1], cuts[1:]):
                acc = _dot(h_scr[...], w_ref[:, a - c0:b - c0])
                fn = activation(a)
                proj_ref[:, a - c0:b - c0] = (acc if fn is None else fn(acc)).astype(BF16)

    @pl.when(j == J_QKV)
    def _():
        acc = _dot(h_scr[...], w_ref[...])
        v0 = C_V - C_Q
        qk_ref[...] = acc[:, :v0]
        if not latent:
            v32_ref[...] = acc[:, v0:]
        vb_ref[...] = acc[:, v0:].astype(BF16)


QK_W = C_V - C_Q
KV_W = N_KV_HEADS * HEAD_DIM


def _in_proj(x, mod4, layer, seq, latent, norm_g, w_main, w_dt):
    t = x.shape[0]
    tm = 1024
    tiles_per_seq = max(seq // tm, 1)
    row_fn = (lambda i, j: LAT_ROW0 + i // tiles_per_seq) if latent else (lambda i, j: CTX_ROW)
    once = lambda width: pl.BlockSpec((tm, width), lambda i, j: (i, 0))
    in_specs = [
        pl.BlockSpec((tm, D_MODEL), lambda i, j: (i, 0)),
        _mod_spec(layer, 0, row_fn),
        _mod_spec(layer, 1, row_fn),
        pl.BlockSpec((1, D_MODEL), lambda i, j: (0, 0)),
        pl.BlockSpec((D_MODEL, PROJ_TN), lambda i, j: (0, j)),
        pl.BlockSpec((D_MODEL, LANES), lambda i, j: (0, 0)),
    ]
    out_shape = [jax.ShapeDtypeStruct((t, C_Q), BF16), jax.ShapeDtypeStruct((t, LANES), F32),
                 jax.ShapeDtypeStruct((t, QK_W), F32), jax.ShapeDtypeStruct((t, KV_W), BF16)]
    out_specs = [pl.BlockSpec((tm, PROJ_TN), lambda i, j: (i, jnp.minimum(j, J_QKV - 1))),
                 once(LANES), once(QK_W), once(KV_W)]
    if not latent:
        out_shape.append(jax.ShapeDtypeStruct((t, KV_W), F32))
        out_specs.append(once(KV_W))
    return pl.pallas_call(
        functools.partial(_inproj_kernel, latent),
        out_shape=out_shape,
        grid=(t // tm, N_MAIN // PROJ_TN),
        in_specs=in_specs,
        out_specs=out_specs,
        scratch_shapes=[pltpu.VMEM((tm, D_MODEL), BF16)],
        compiler_params=_cparams(("parallel", "arbitrary")),
        name="in_proj_lat" if latent else "in_proj_ctx",
    )(x, mod4, mod4, norm_g, w_main, w_dt)


def _attn_kernel(latent, q_ref, k_ref, v_ref, qg_ref, kg_ref, *rest):
    if latent:
        cosq_ref, sinq_ref, cosk_ref, sink_ref, ck_ref, cv_ref, o_ref, k_scr, v1_scr = rest
    else:
        has_prev = len(rest) == 7
        o_ref, kn_ref, vn_ref, k_scr, v1_scr = rest[-5:]
        k_slab = kn_ref if has_prev else kn_ref.at[0]
        v_slab = vn_ref if has_prev else vn_ref.at[0]
    seq = k_ref.shape[0]
    nt = (((1,), (1,)), ((), ()))

    @pl.when(pl.program_id(2) == 0)
    def _():
        k = _head_norm(k_ref[...], kg_ref[...])
        v = v_ref[...]
        if latent:
            k = _rope(k, cosk_ref[...], sink_ref[...])
            k_scr[seq:, :] = ck_ref[...].astype(BF16)
            v1_scr[seq:, 0:HEAD_DIM] = cv_ref[...].astype(BF16)
        else:
            g = pl.program_id(1)
            if not has_prev:
                @pl.when(g == 0)
                def _():
                    kn_ref[1:] = jnp.zeros((DEPTH - 1,) + tuple(kn_ref.shape[1:]), F32)
                    vn_ref[1:] = jnp.zeros((DEPTH - 1,) + tuple(vn_ref.shape[1:]), F32)
            for gg in range(N_KV_HEADS):
                @pl.when(g == gg)
                def _(gg=gg):
                    k_slab[:, gg, :] = k
                    v_slab[:, gg, :] = v
        k_scr[0:seq, :] = k.astype(BF16)
        v1_scr[0:seq, 0:HEAD_DIM] = v.astype(BF16)
        v1_scr[:, HEAD_DIM:] = jnp.ones((v1_scr.shape[0], HEAD_DIM), BF16)

    q_scale = HEAD_DIM ** -0.5 * LOG2E
    for hh in range(KV_REP):
        sl = slice(hh * HEAD_DIM, (hh + 1) * HEAD_DIM)
        q = _head_norm(q_ref[:, sl], qg_ref[...])
        if latent:
            q = _rope(q, cosq_ref[...], sinq_ref[...])
        s = lax.dot_general((q * q_scale).astype(BF16), k_scr[...], nt, preferred_element_type=F32)
        m = jnp.max(s, axis=-1, keepdims=True)
        o = _dot(jnp.exp2(s - m).astype(BF16), v1_scr[...])
        o_ref[:, sl] = (o[:, :HEAD_DIM] / o[:, HEAD_DIM:]).astype(BF16)


def _attention(qk, v, seq, latent, layer, q_g, k_g, rope_tabs, cache_k4, cache_v4, kv_all=None):
    t = qk.shape[0]
    nb = t // seq
    tq = min(1024, seq)
    qt = seq // tq
    gw = KV_REP * HEAD_DIM
    const = lambda b, g, i: (0, 0)
    in_specs = [
        pl.BlockSpec((tq, gw), lambda b, g, i: (b * qt + i, g)),
        pl.BlockSpec((seq, HEAD_DIM), lambda b, g, i: (b, N_HEADS + g)),
        pl.BlockSpec((seq, HEAD_DIM), lambda b, g, i: (b, g)),
        pl.BlockSpec((1, HEAD_DIM), const), pl.BlockSpec((1, HEAD_DIM), const),
    ]
    args = [qk, qk, v, q_g, k_g]
    past = cache_k4.shape[2] if latent else 0
    out_shape = [jax.ShapeDtypeStruct((t, N_HEADS * HEAD_DIM), BF16)]
    out_specs = [pl.BlockSpec((tq, gw), lambda b, g, i: (b * qt + i, g))]
    aliases = {}
    if latent:
        tab_q = pl.BlockSpec((tq, HEAD_DIM), lambda b, g, i: (i, 0))
        tab_k = pl.BlockSpec((seq, HEAD_DIM), const)
        cspec = pl.BlockSpec((None, None, past, HEAD_DIM), lambda b, g, i: (b, layer, 0, g))
        in_specs += [tab_q, tab_q, tab_k, tab_k, cspec, cspec]
        args += [rope_tabs[0], rope_tabs[1], rope_tabs[0], rope_tabs[1], cache_k4, cache_v4]
    else:
        kv_shape = (nb, DEPTH, seq, N_KV_HEADS, HEAD_DIM)
        out_shape += [jax.ShapeDtypeStruct(kv_shape, F32)] * 2
        if kv_all is not None:
            aliases = {len(args): 1, len(args) + 1: 2}
            in_specs += [pl.BlockSpec(memory_space=pl.ANY)] * 2
            args += list(kv_all)
            slab = pl.BlockSpec((None, None, seq, N_KV_HEADS, HEAD_DIM), lambda b, g, i: (b, layer, 0, 0, 0))
        else:
            assert layer == 0
            slab = pl.BlockSpec((None, DEPTH, seq, N_KV_HEADS, HEAD_DIM), lambda b, g, i: (b, 0, 0, 0, 0))
        out_specs += [slab, slab]
    return pl.pallas_call(
        functools.partial(_attn_kernel, latent),
        out_shape=out_shape,
        grid=(nb, N_KV_HEADS, qt),
        in_specs=in_specs,
        out_specs=out_specs,
        scratch_shapes=[pltpu.VMEM((seq + past, HEAD_DIM), BF16), pltpu.VMEM((seq + past, 2 * HEAD_DIM), BF16)],
        input_output_aliases=aliases,
        compiler_params=_cparams(("parallel", "parallel" if latent else "arbitrary", "arbitrary")),
        name="attention_lat" if latent else "attention_ctx",
    )(*args)


def _conv3(p, prev_row, next_row, w, first, last):
    tm = p.shape[0]
    sub = 8
    rid = lax.broadcasted_iota(jnp.int32, (sub, p.shape[1]), 0)
    prev_row = jnp.where(first, 0.0, prev_row)
    next_row = jnp.where(last, 0.0, next_row)
    dn = pltpu.roll(p, 1, 0)
    up = pltpu.roll(p, tm - 1, 0)
    dn = jnp.concatenate([jnp.where(rid == 0, prev_row, dn[:sub, :]), dn[sub:, :]], axis=0)
    up = jnp.concatenate([up[:tm - sub, :], jnp.where(rid == sub - 1, next_row, up[tm - sub:, :])], axis=0)
    return w[0:1, :] * dn + w[1:2, :] * p + w[2:3, :] * up


GW = SSD_HPG * SSD_HEADDIM
NBC = N_SSD_GROUPS * D_STATE
CONV_SLAB = 1024


def _ssd_chunk(rev, xcols, dtr, dtr_t, brow_ref, bcol_ref, arow_ref, acol_ref, st_scr, d_ref, emit):
    ii = lax.broadcasted_iota(jnp.int32, (CHUNK, CHUNK), 0)
    jj = lax.broadcasted_iota(jnp.int32, (CHUNK, CHUNK), 1)
    keep = (ii <= jj) if rev else (ii >= jj)
    tri = jnp.where(keep, 1.0, 0.0).astype(BF16)
    tri_t = jnp.where((jj <= ii) if rev else (jj >= ii), 1.0, 0.0).astype(BF16)

    dt = _softplus(dtr + brow_ref[...])
    acs = sum(_dot(tri, part) for part in _split3(dt * (-jnp.exp(arow_ref[...]))))
    dt_t = _softplus(dtr_t + bcol_ref[...])
    acs_t = sum(_dot(part, tri_t) for part in _split3(dt_t * (-jnp.exp(acol_ref[...]))))

    edge = 0 if rev else CHUNK - 1
    ea = jnp.exp(acs)
    dec_row = ea[edge:edge + 1, :]
    w_t = dt_t * jnp.exp(acs_t[:, edge:edge + 1] - acs_t)
    acs2 = acs * LOG2E
    l2_t = acs_t * LOG2E - jnp.log2(dt_t)
    head_of_lane = lax.broadcasted_iota(jnp.int32, (CHUNK, GW), 1) // SSD_HEADDIM
    head_of_lane_row = lax.broadcasted_iota(jnp.int32, (1, GW), 1) // SSD_HEADDIM
    lane0 = N_SSD_HEADS if rev else 0
    zero_blk = jnp.zeros((D_STATE, CHUNK), BF16)

    for g in range(N_SSD_GROUPS):
        gs = slice(g * GW, (g + 1) * GW)
        xs = xcols(g * GW, (g + 1) * GW)
        bc = xcols(D_INNER + g * D_STATE, D_INNER + (g + 1) * D_STATE)
        cc = xcols(D_INNER + NBC + g * D_STATE, D_INNER + NBC + (g + 1) * D_STATE)
        bc_t = bc.astype(F32).T
        cc_f = cc.astype(F32)
        cb = _dot(cc, bc_t.astype(BF16))
        st = st_scr[:, gs]
        st_b = st.astype(BF16)
        acc = None
        dec_w = jnp.zeros((1, GW), F32)
        for r in range(SSD_HPG):
            ln = lane0 + g * SSD_HPG + r
            seg = acs2[:, ln:ln + 1] - l2_t[ln:ln + 1, :]
            m = cb * jnp.exp2(jnp.where(keep, seg, -jnp.inf))
            cce = cc_f * ea[:, ln:ln + 1]
            bw = bc_t * w_t[ln:ln + 1, :]
            lhs = jnp.concatenate([jnp.concatenate([m.astype(BF16), cce.astype(BF16)], axis=1),
                                   jnp.concatenate([bw.astype(BF16), zero_blk], axis=1)], axis=0)
            sel = head_of_lane == r
            rhs = jnp.concatenate([jnp.where(sel, xs, jnp.zeros_like(xs)),
                                   jnp.where(sel, st_b, jnp.zeros_like(st_b))], axis=0)
            part = _dot(lhs, rhs)
            acc = part if acc is None else acc + part
            dec_w = jnp.where(head_of_lane_row == r, dec_row[:, ln:ln + 1], dec_w)
        y = acc[:CHUNK, :]
        if not rev:
            y = y + d_ref[:, gs] * xs.astype(F32)
        emit(g, y)
        st_scr[:, gs] = st * dec_w + acc[CHUNK:, :]


SSD_MAX_CPS = 4


def _ssd_kernel(nsteps, cps, has_init, has_prev, emit_final, xbc_ref, xp_ref, xn_ref, cw_ref, cb_ref, dtr_ref,
                dtrt_ref, brow_ref, bcol_ref, arow_ref, acol_ref, d_ref, *rest):
    rest = list(rest)
    s0_ref = rest.pop(0) if has_init else None
    if has_prev:
        rest.pop(0)
    y_ref = rest.pop(0)
    sfin_ref = rest.pop(0) if emit_final else None
    xact_scr, yf_scr, st_scr = rest
    s = pl.program_id(1)
    scan_args = (brow_ref, bcol_ref, arow_ref, acol_ref, st_scr, d_ref)

    def scan(rev, blk, k, emit):
        rows = slice(k * CHUNK, (k + 1) * CHUNK)
        _ssd_chunk(rev, lambda lo, hi: xact_scr[blk, rows, lo:hi], dtr_ref[rows, :], dtrt_ref[:, rows],
                   *scan_args, functools.partial(emit, rows))

    def load_state(d):
        if not has_init:
            st_scr[...] = jnp.zeros_like(st_scr)
            return
        for k in range(D_INNER // LANES):
            sl = slice(k * LANES, (k + 1) * LANES)
            st_scr[:, sl] = s0_ref[d, sl, :].T

    slab_ref = None
    if emit_final:
        slab_ref = sfin_ref if has_prev else sfin_ref.at[0]

    def store_state(d):
        if not emit_final:
            return
        for k in range(D_INNER // LANES):
            sl = slice(k * LANES, (k + 1) * LANES)
            slab_ref[d, sl, :] = st_scr[:, sl].T

    @pl.when(s == 0)
    def _():
        load_state(0)
        if emit_final and not has_prev:
            sfin_ref[1:] = jnp.zeros((DEPTH - 1,) + tuple(sfin_ref.shape[1:]), F32)

    @pl.when(s < nsteps)
    def _():
        blk = s
        r = BF16_ROWS - 1
        for k in range(XBC_W // CONV_SLAB):
            sl = slice(k * CONV_SLAB, (k + 1) * CONV_SLAB)
            conv = _conv3(xbc_ref[:, sl].astype(F32), xp_ref[r:r + 1, sl].astype(F32),
                          xn_ref[0:1, sl].astype(F32), cw_ref[:, sl], s == 0, s == nsteps - 1)
            xact_scr[blk, :, sl] = _silu(conv + cb_ref[:, sl]).astype(BF16)

        def emit(rows, g, y):
            yf_scr[blk, rows, g * GW:(g + 1) * GW] = y.astype(BF16)

        for k in range(cps):
            scan(False, blk, k, emit)

    @pl.when(s == nsteps - 1)
    def _():
        store_state(0)

    @pl.when(s == nsteps)
    def _():
        load_state(1)

    @pl.when(s >= nsteps)
    def _():
        blk = 2 * nsteps - 1 - s

        def emit(rows, g, y):
            gs = slice(g * GW, (g + 1) * GW)
            y_ref[rows, gs] = (y + yf_scr[blk, rows, gs].astype(F32)).astype(BF16)

        for k in reversed(range(cps)):
            scan(True, blk, k, emit)

    @pl.when(s == 2 * nsteps - 1)
    def _():
        store_state(1)


def _ssd(proj, dt_raw, dt_raw_t, seq, conv_w, conv_b, bias, a_log, d_wide, layer, s0, emit_final, sfin_all):
    t = proj.shape[0]
    nb = t // seq
    cps = min(SSD_MAX_CPS, seq // CHUNK)
    rows = cps * CHUNK
    nchunks = nsteps = seq // rows
    rpc = rows // BF16_ROWS
    last_blk = t // BF16_ROWS - 1
    xb = C_XBC // XBC_W

    def chunk_in(b, s):
        return b * nchunks + jnp.minimum(s, nchunks - 1)

    def chunk_of(b, s):
        return b * nchunks + jnp.where(s < nchunks, s, 2 * nchunks - 1 - s)

    def chunk_out(b, s):
        return b * nchunks + jnp.where(s < nchunks, nchunks - 1, 2 * nchunks - 1 - s)

    pad = jnp.zeros((LANES - 2 * N_SSD_HEADS,), F32)
    b_row = jnp.concatenate([bias.reshape(-1), pad]).reshape(1, LANES)
    a_row = jnp.concatenate([a_log.reshape(-1), pad]).reshape(1, LANES)
    const = lambda b, s: (0, 0)
    in_specs = [
        pl.BlockSpec((rows, XBC_W), lambda b, s: (chunk_in(b, s), xb)),
        pl.BlockSpec((BF16_ROWS, XBC_W), lambda b, s: (jnp.maximum(chunk_in(b, s) * rpc - 1, 0), xb)),
        pl.BlockSpec((BF16_ROWS, XBC_W), lambda b, s: (jnp.minimum((chunk_in(b, s) + 1) * rpc, last_blk), xb)),
        pl.BlockSpec((CONV_K, XBC_W), const),
        pl.BlockSpec((1, XBC_W), const),
        pl.BlockSpec((rows, LANES), lambda b, s: (chunk_of(b, s), 0)),
        pl.BlockSpec((LANES, rows), lambda b, s: (0, chunk_of(b, s))),
        pl.BlockSpec((1, LANES), const), pl.BlockSpec((LANES, 1), const),
        pl.BlockSpec((1, LANES), const), pl.BlockSpec((LANES, 1), const),
        pl.BlockSpec((1, D_INNER), const),
    ]
    args = [proj, proj, proj, conv_w, conv_b, dt_raw, dt_raw_t, b_row, b_row.reshape(LANES, 1), a_row,
            a_row.reshape(LANES, 1), d_wide]
    out_shape = [jax.ShapeDtypeStruct((t, D_INNER), BF16)]
    out_specs = [pl.BlockSpec((rows, D_INNER), lambda b, s: (chunk_out(b, s), 0))]
    state_spec = pl.BlockSpec((None, None, 2, D_INNER, D_STATE), lambda b, s: (b, layer, 0, 0, 0))
    aliases = {}
    if s0 is not None:
        in_specs.append(state_spec)
        args.append(s0)
    if emit_final:
        out_shape.append(jax.ShapeDtypeStruct((nb, DEPTH, 2, D_INNER, D_STATE), F32))
        if sfin_all is not None:
            aliases = {len(args): 1}
            in_specs.append(pl.BlockSpec(memory_space=pl.ANY))
            args.append(sfin_all)
            out_specs.append(state_spec)
        else:
            assert layer == 0
            out_specs.append(pl.BlockSpec((None, DEPTH, 2, D_INNER, D_STATE), lambda b, s: (b, 0, 0, 0, 0)))
    return pl.pallas_call(
        functools.partial(_ssd_kernel, nsteps, cps, s0 is not None, bool(aliases), emit_final),
        out_shape=out_shape,
        grid=(nb, 2 * nsteps),
        in_specs=in_specs,
        out_specs=out_specs,
        input_output_aliases=aliases,
        scratch_shapes=[pltpu.VMEM((nsteps, rows, XBC_W), BF16), pltpu.VMEM((nsteps, rows, D_INNER), BF16),
                        pltpu.VMEM((D_STATE, D_INNER), F32)],
        compiler_params=_cparams(("parallel", "arbitrary")),
        name="ssd_lat" if s0 is not None else "ssd_ctx",
    )(*args)


def _merge_kernel(tm, seq, x_ref, gate_ref, attn_ref, bg_ref, cg_ref, cx_ref, cgp_ref, cxp_ref, cgn_ref, cxn_ref,
                  cw_ref, y_ref, z_ref, gl_ref, ng_ref, wa_ref, wc_ref, ws_ref, wm_ref, o_ref):
    i = pl.program_id(0)
    tps = seq // tm
    r = BF16_ROWS - 1
    p = cg_ref[...].astype(F32) * cx_ref[...].astype(F32)
    prev_row = cgp_ref[r:r + 1, :].astype(F32) * cxp_ref[r:r + 1, :].astype(F32)
    next_row = cgn_ref[0:1, :].astype(F32) * cxn_ref[0:1, :].astype(F32)
    conv = _conv3(p, prev_row, next_row, cw_ref[...], (i % tps) == 0, (i % tps) == tps - 1)
    u = (bg_ref[...].astype(F32) * conv).astype(BF16)
    tg = y_ref[...].astype(F32) * z_ref[...].astype(F32)
    ms = jnp.mean(tg * tg, axis=-1, keepdims=True)
    yn = (tg * lax.rsqrt(ms + EPS) * ng_ref[...]).astype(BF16)
    y_ssd = _dot(yn, ws_ref[...])
    y_attn = _dot(attn_ref[...], wa_ref[...])
    y_conv = _dot(u, wc_ref[...])
    d = D_MODEL
    mix = (gl_ref[:, 0:d].astype(F32) * y_attn + gl_ref[:, d:2 * d].astype(F32) * y_conv
           + gl_ref[:, 2 * d:3 * d].astype(F32) * y_ssd)
    o_ref[...] = x_ref[...] + gate_ref[...] * _dot(mix.astype(BF16), wm_ref[...])


def _resident(shape):
    return pl.BlockSpec(shape, lambda *_: (0,) * len(shape), pipeline_mode=pl.Buffered(1))


def _merge(x, mod4, layer, seq, latent, proj, attn, y, conv_w, norm_g, wa, wc, ws, wm):
    t = x.shape[0]
    tm = min(512, seq)
    tps = seq // tm
    row_fn = (lambda i: LAT_ROW0 + i // tps) if latent else (lambda i: CTX_ROW)
    d = D_MODEL
    rpt = tm // BF16_ROWS
    last_blk = t // BF16_ROWS - 1

    def tile(width, col_block):
        return pl.BlockSpec((tm, width), lambda i: (i, col_block))

    def prev(cb):
        return pl.BlockSpec((BF16_ROWS, d), lambda i: (jnp.maximum(i * rpt - 1, 0), cb))

    def nxt(cb):
        return pl.BlockSpec((BF16_ROWS, d), lambda i: (jnp.minimum((i + 1) * rpt, last_blk), cb))

    return pl.pallas_call(
        functools.partial(_merge_kernel, tm, seq),
        out_shape=jax.ShapeDtypeStruct((t, d), F32),
        grid=(t // tm,),
        in_specs=[
            tile(d, 0),
            _mod_spec(layer, 2, row_fn),
            tile(d, 0),
            tile(d, C_BG // d), tile(d, C_CG // d), tile(d, C_CX // d),
            prev(C_CG // d), prev(C_CX // d), nxt(C_CG // d), nxt(C_CX // d),
            _resident((CONV_K, d)),
            tile(D_INNER, 0),
            tile(D_INNER, C_Z // D_INNER),
            tile(3 * d, C_GATE // (3 * d)),
            _resident((1, D_INNER)),
            _resident((d, d)), _resident((d, d)), _resident((D_INNER, d)), _resident((d, d)),
        ],
        out_specs=tile(d, 0),
        compiler_params=_cparams(("parallel",)),
        name="merge",
    )(x, mod4, attn, proj, proj, proj, proj, proj, proj, proj, conv_w, y, proj, proj, norm_g, wa, wc, ws, wm)


FF_SPLIT = 11


def _ffn_kernel(x_ref, shift_ref, scale_ref, gate_ref, g_ref, w1_ref, w2_ref, o_ref):
    x = x_ref[...]
    h = _norm_mod(x, g_ref[...], scale_ref[...], shift_ref[...]).astype(BF16)
    slab = D_FF // FF_SPLIT
    acc = None
    for j in range(FF_SPLIT):
        hg = _dot(h, w1_ref[:, j * slab:(j + 1) * slab])
        hu = _dot(h, w1_ref[:, D_FF + j * slab:D_FF + (j + 1) * slab])
        part = _dot((_silu(hg) * hu).astype(BF16), w2_ref[j * slab:(j + 1) * slab, :])
        acc = part if acc is None else acc + part
    o_ref[...] = x + gate_ref[...] * acc


def _ffn(x, mod4, layer, seq, latent, norm_g, w1, w2):
    t = x.shape[0]
    tm = 1024
    tps = max(seq // tm, 1)
    row_fn = (lambda i: LAT_ROW0 + i // tps) if latent else (lambda i: CTX_ROW)
    return pl.pallas_call(
        _ffn_kernel,
        out_shape=jax.ShapeDtypeStruct((t, D_MODEL), F32),
        grid=(t // tm,),
        in_specs=[
            pl.BlockSpec((tm, D_MODEL), lambda i: (i, 0)),
            _mod_spec(layer, 3, row_fn),
            _mod_spec(layer, 4, row_fn),
            _mod_spec(layer, 5, row_fn),
            _resident((1, D_MODEL)),
            _resident((D_MODEL, 2 * D_FF)),
            _resident((D_FF, D_MODEL)),
        ],
        out_specs=pl.BlockSpec((tm, D_MODEL), lambda i: (i, 0)),
        compiler_params=_cparams(("parallel",)),
        name="ffn",
    )(x, mod4, mod4, mod4, norm_g, w1, w2)


def _rope_tables(n_tok):
    rows = n_tok // GRID_W
    row = jnp.repeat(jnp.arange(rows, dtype=F32), GRID_W)
    col = jnp.tile(jnp.arange(GRID_W, dtype=F32), rows)
    axis_dim = HEAD_DIM // 2
    inv_freq = 1.0 / (ROPE_THETA ** (jnp.arange(0, axis_dim, 2, dtype=F32) / axis_dim))
    ang = jnp.concatenate([row[:, None] * inv_freq, col[:, None] * inv_freq], axis=-1)
    cos = jnp.repeat(jnp.cos(ang), 2, axis=-1)
    sin = jnp.repeat(jnp.sin(ang), 2, axis=-1)
    sign = jnp.tile(jnp.asarray([-1.0, 1.0], F32), HEAD_DIM // 2)
    return cos, sin * sign


def _layer(x, seq, latent, layer, mod4, wts, cache_k4, cache_v4, s0, rope_tabs, kv_all=None, sfin_all=None):
    proj, dt_raw, qk, vb, *v32 = _in_proj(x, mod4, layer, seq, latent, wts["norm1_g"], wts["w_main"], wts["w_dt"])
    attn, *kv = _attention(qk, vb if latent else v32[0], seq, latent, layer, wts["q_g"], wts["k_g"], rope_tabs,
                           cache_k4, cache_v4, kv_all)
    y, *s_fin = _ssd(proj, dt_raw, dt_raw.T, seq, wts["ssd_conv_w"], wts["ssd_conv_b"], wts["dt_bias"],
                     wts["a_log"], wts["d_wide"], layer, s0, not latent, sfin_all)
    x = _merge(x, mod4, layer, seq, latent, proj, attn, y, wts["conv_w"], wts["ssd_norm_g"],
               wts["w_attn_o"], wts["w_conv_o"], wts["w_ssd_o"], wts["w_merge"])
    x = _ffn(x, mod4, layer, seq, latent, wts["norm2_g"], wts["ffn_w1"], wts["ffn_w2"])
    return x, kv, s_fin


def _layer_weights(l, norm1_g, norm2_g, w_in, q_norm_g, k_norm_g, w_attn_o, conv_w, w_conv_o, ssd_conv_w,
                   ssd_conv_b, ssd_dt_bias, ssd_a_log, ssd_d, ssd_norm_g, w_ssd_o, w_merge, ffn_w1, ffn_w2):
    d = D_MODEL
    kvw = N_KV_HEADS * HEAD_DIM
    o_q, o_k, o_v, o_bg, o_cg, o_cx = 0, d, d + kvw, d + 2 * kvw, 2 * d + 2 * kvw, 3 * d + 2 * kvw
    o_z = o_cx + d
    o_xbc = o_z + D_INNER
    o_dt = o_xbc + XBC_W
    o_gate = o_dt + 2 * N_SSD_HEADS
    w = w_in[l]
    cols = lambda a, n: w[:, a:a + n]
    w_main = jnp.concatenate([cols(o_xbc, XBC_W), cols(o_z, D_INNER), cols(o_gate, 3 * d), cols(o_bg, d),
                              cols(o_cg, d), cols(o_cx, d), cols(o_q, d), cols(o_k, kvw), cols(o_v, kvw)],
                             axis=1).astype(BF16)
    w_dt = jnp.concatenate([cols(o_dt, 2 * N_SSD_HEADS), jnp.zeros((d, LANES - 2 * N_SSD_HEADS), F32)],
                           axis=1).astype(BF16)
    return {
        "norm1_g": norm1_g[l].reshape(1, d), "norm2_g": norm2_g[l].reshape(1, d),
        "w_main": w_main, "w_dt": w_dt,
        "q_g": q_norm_g[l].reshape(1, HEAD_DIM), "k_g": k_norm_g[l].reshape(1, HEAD_DIM),
        "w_attn_o": w_attn_o[l].astype(BF16), "conv_w": conv_w[l], "w_conv_o": w_conv_o[l].astype(BF16),
        "ssd_conv_w": ssd_conv_w[l], "ssd_conv_b": ssd_conv_b[l].reshape(1, -1),
        "dt_bias": ssd_dt_bias[l], "a_log": ssd_a_log[l],
        "d_wide": jnp.repeat(ssd_d[l], SSD_HEADDIM).reshape(1, D_INNER),
        "ssd_norm_g": ssd_norm_g[l].reshape(1, D_INNER), "w_ssd_o": w_ssd_o[l].astype(BF16),
        "w_merge": w_merge[l].astype(BF16), "ffn_w1": ffn_w1[l].astype(BF16), "ffn_w2": ffn_w2[l].astype(BF16),
    }


def kernel(x_prompt, x_sample, c, cache_k, cache_v, state_ssd, c_ctx, ada_w, ada_b, norm1_g, norm2_g, w_in,
           q_norm_g, k_norm_g, w_attn_o, conv_w, w_conv_o, ssd_conv_w, ssd_conv_b, ssd_dt_bias, ssd_a_log, ssd_d,
           ssd_norm_g, w_ssd_o, w_merge, ffn_w1, ffn_w2):
    nb_ctx, seq_ctx, d = x_prompt.shape
    nb_lat, seq_lat, _ = x_sample.shape
    past = cache_k.shape[2]
    cond = jnp.concatenate([c_ctx[None, :], c, jnp.zeros((MOD_ROWS - nb_lat - 1, d), F32)], axis=0)
    mod4 = _modulation(cond, ada_w, ada_b).reshape(DEPTH, MOD_ROWS, 1, 6 * d)
    cache_k4 = cache_k.reshape(nb_lat, DEPTH, past, N_KV_HEADS * HEAD_DIM)
    cache_v4 = cache_v.reshape(nb_lat, DEPTH, past, N_KV_HEADS * HEAD_DIM)
    rope_tabs = _rope_tables(seq_lat)
    s0_lat = state_ssd.reshape(nb_lat, DEPTH, 2, D_INNER, D_STATE)

    y_ctx = x_prompt.reshape(nb_ctx * seq_ctx, d)
    y_lat = x_sample.reshape(nb_lat * seq_lat, d)
    kv_all = sfin_all = None
    for l in range(DEPTH):
        wts = _layer_weights(l, norm1_g, norm2_g, w_in, q_norm_g, k_norm_g, w_attn_o, conv_w, w_conv_o,
                             ssd_conv_w, ssd_conv_b, ssd_dt_bias, ssd_a_log, ssd_d, ssd_norm_g, w_ssd_o,
                             w_merge, ffn_w1, ffn_w2)
        y_ctx, kv_all, (sfin_all,) = _layer(y_ctx, seq_ctx, False, l, mod4, wts, None, None, None, None,
                                            kv_all, sfin_all)
        y_lat, _, _ = _layer(y_lat, seq_lat, True, l, mod4, wts, cache_k4, cache_v4, s0_lat, rope_tabs)
    new_k, new_v = kv_all
    new_ssd = sfin_all.reshape(nb_ctx, DEPTH, 2, N_SSD_HEADS, SSD_HEADDIM, D_STATE)
    return (y_ctx.reshape(nb_ctx, seq_ctx, d), y_lat.reshape(nb_lat, seq_lat, d), new_k, new_v, new_ssd)
```
